```python
import math
import jax, jax.numpy as jnp
from jax import lax
import numpy as np

D_MODEL = 2048
BATCH = 8
SEQ = 2048
DEPTH = 2
DEC_BATCH = 32
DEC_SEQ = 8
PAST_LEN = 8192
PAGE_SIZE = 128

N_A_LAYERS = DEPTH // 2
N_B_LAYERS = DEPTH - N_A_LAYERS
CONV_WIDTH = 31
N_HEADS = 8
N_KV_HEADS = 8
HEAD_DIM = D_MODEL // N_HEADS // 2
V_HEAD_DIM = 2 * HEAD_DIM
D_FF = ((8 * D_MODEL + 767) // 768) * 256
Q_BLOCK = 128
RMS_EPS = 1e-6
SUBLN_EPS = 1e-5
LN_EPS = 1e-5
NEG_INF = -1e30

kernel_name = "yoco_conformer_conv_diff_attn_step"


def _rmsnorm(x, g, eps=RMS_EPS):
    xf = x.astype(jnp.float32)
    y = xf * lax.rsqrt(jnp.mean(xf * xf, axis=-1, keepdims=True) + eps)
    return (y * g.astype(jnp.float32)).astype(x.dtype)


def _layernorm(x, g, b):
    xf = x.astype(jnp.float32)
    mu = jnp.mean(xf, axis=-1, keepdims=True)
    var = jnp.mean(jnp.square(xf - mu), axis=-1, keepdims=True)
    y = (xf - mu) * lax.rsqrt(var + LN_EPS)
    return (y * g.astype(jnp.float32) + b.astype(jnp.float32)).astype(x.dtype)


def _swiglu(h, w_gate, w_up, w_down):
    return (jax.nn.silu(h @ w_gate) * (h @ w_up)) @ w_down


def _conv_module(h, state, w_pw1, b_pw1, w_dw, b_dw, ln_g, ln_b, w_pw2, b_pw2):
    a = h @ w_pw1 + b_pw1
    u = a[..., :D_MODEL] * jax.nn.sigmoid(a[..., D_MODEL:])
    full = jnp.concatenate([state.astype(u.dtype), u], axis=1)
    c = lax.conv_general_dilated(
        full, w_dw[:, None, :].astype(full.dtype), window_strides=(1,), padding='VALID',
        dimension_numbers=('NWC', 'WIO', 'NWC'), feature_group_count=D_MODEL) + b_dw
    new_state = full[:, full.shape[1] - (CONV_WIDTH - 1):]
    s = jax.nn.silu(_layernorm(c, ln_g, ln_b))
    return s @ w_pw2 + b_pw2, new_state


def _expand_kv(k, v):
    rep = N_HEADS // N_KV_HEADS
    if rep == 1:
        return k, v
    return jnp.repeat(k, rep, axis=2), jnp.repeat(v, rep, axis=2)


def _diff_core(q, k, v, q_pos, k_pos, lam):
    k, v = _expand_kv(k, v)
    s = jnp.einsum('bqhcd,bkhcd->bhcqk', q, k).astype(jnp.float32) * (HEAD_DIM ** -0.5)
    s = jnp.where(k_pos[None, :] <= q_pos[:, None], s, NEG_INF)
    p = jax.nn.softmax(s, axis=-1)
    a = p[:, :, 0] - lam * p[:, :, 1]
    return jnp.einsum('bhqk,bkhe->bqhe', a.astype(v.dtype), v)


def _diff_attention(h, k, v, attend, w_q, lq1, lk1, lq2, lk2, subln_g, w_o, layer_idx):
    B, T, _ = h.shape
    q = (h @ w_q).reshape(B, T, N_HEADS, 2, HEAD_DIM)
    lam_init = 0.8 - 0.6 * math.exp(-0.3 * layer_idx)
    lam = (jnp.exp(jnp.sum(lq1.astype(jnp.float32) * lk1.astype(jnp.float32)))
           - jnp.exp(jnp.sum(lq2.astype(jnp.float32) * lk2.astype(jnp.float32))) + lam_init)
    o = attend(q, k, v, lam)
    o = _rmsnorm(o, subln_g, SUBLN_EPS) * (1.0 - lam_init)
    return o.reshape(B, T, N_HEADS * V_HEAD_DIM) @ w_o


def _trunk(x, conv_state, attend, p):
    B, T, _ = x.shape
    new_conv = []
    k = None
    v = None
    for i in range(DEPTH):
        h = _rmsnorm(x, p['attn_norm_g'][i])
        if i < N_A_LAYERS:
            y, st = _conv_module(h, conv_state[i], p['conv_w_pw1'][i], p['conv_b_pw1'][i],
                                 p['conv_w_dw'][i], p['conv_b_dw'][i], p['conv_ln_g'][i],
                                 p['conv_ln_b'][i], p['conv_w_pw2'][i], p['conv_b_pw2'][i])
            new_conv.append(st)
        else:
            if i == N_A_LAYERS:
                kv_in = _rmsnorm(x, p['kv_norm_g'])
                k = (kv_in @ p['w_k']).reshape(B, T, N_KV_HEADS, 2, HEAD_DIM)
                v = (kv_in @ p['w_v']).reshape(B, T, N_KV_HEADS, V_HEAD_DIM)
            j = i - N_A_LAYERS
            y = _diff_attention(h, k, v, attend, p['attn_w_q'][j], p['attn_lambda_q1'][j],
                                p['attn_lambda_k1'][j], p['attn_lambda_q2'][j],
                                p['attn_lambda_k2'][j], p['attn_subln_g'][j], p['attn_w_o'][j], i)
        x = x + y
        x = x + _swiglu(_rmsnorm(x, p['ffn_norm_g'][i]), p['ffn_w_gate'][i],
                        p['ffn_w_up'][i], p['ffn_w_down'][i])
    y_out = _rmsnorm(x, p['final_norm_g'])
    k_rows = k.reshape(B, T, 2 * N_KV_HEADS, HEAD_DIM)
    return y_out, jnp.stack(new_conv, axis=0), k_rows, v


def setup_inputs(seed: int = 0) -> dict:
    key = jax.random.key(seed)
    ks = jax.random.split(key, 40)
    f32 = jnp.float32
    n_pages = PAST_LEN // PAGE_SIZE
    n_used = DEC_BATCH * n_pages
    n_pool = n_used + n_used // 4
    nrm = lambda k, shape, s: jax.random.normal(k, shape, f32) * s
    d_inv = D_MODEL ** -0.5
    page_table = jax.random.permutation(ks[5], n_pool)[:n_used].reshape(DEC_BATCH, n_pages).astype(jnp.int32)
    return {
        'x_prompt': nrm(ks[0], (BATCH, SEQ, D_MODEL), 1.0),
        'x_sample': nrm(ks[1], (DEC_BATCH, DEC_SEQ, D_MODEL), 1.0),
        'state_conv': nrm(ks[2], (N_A_LAYERS, DEC_BATCH, CONV_WIDTH - 1, D_MODEL), 0.5),
        'cache_k': nrm(ks[3], (n_pool, PAGE_SIZE, 2 * N_KV_HEADS, HEAD_DIM), 1.0),
        'cache_v': nrm(ks[4], (n_pool, PAGE_SIZE, N_KV_HEADS, V_HEAD_DIM), 1.0),
        'page_table': page_table,
        'attn_norm_g': 1.0 + nrm(ks[6], (DEPTH, D_MODEL), 0.02),
        'ffn_norm_g': 1.0 + nrm(ks[7], (DEPTH, D_MODEL), 0.02),
        'conv_w_pw1': nrm(ks[8], (N_A_LAYERS, D_MODEL, 2 * D_MODEL), d_inv),
        'conv_b_pw1': nrm(ks[9], (N_A_LAYERS, 2 * D_MODEL), 0.02),
        'conv_w_dw': nrm(ks[10], (N_A_LAYERS, CONV_WIDTH, D_MODEL), CONV_WIDTH ** -0.5),
        'conv_b_dw': nrm(ks[11], (N_A_LAYERS, D_MODEL), 0.02),
        'conv_ln_g': 1.0 + nrm(ks[12], (N_A_LAYERS, D_MODEL), 0.02),
        'conv_ln_b': nrm(ks[13], (N_A_LAYERS, D_MODEL), 0.02),
        'conv_w_pw2': nrm(ks[14], (N_A_LAYERS, D_MODEL, D_MODEL), d_inv),
        'conv_b_pw2': nrm(ks[15], (N_A_LAYERS, D_MODEL), 0.02),
        'kv_norm_g': 1.0 + nrm(ks[16], (D_MODEL,), 0.02),
        'w_k': nrm(ks[17], (D_MODEL, 2 * N_KV_HEADS * HEAD_DIM), d_inv),
        'w_v': nrm(ks[18], (D_MODEL, N_KV_HEADS * V_HEAD_DIM), d_inv),
        'attn_w_q': nrm(ks[19], (N_B_LAYERS, D_MODEL, 2 * N_HEADS * HEAD_DIM), d_inv),
        'attn_lambda_q1': nrm(ks[20], (N_B_LAYERS, HEAD_DIM), 0.1),
        'attn_lambda_k1': nrm(ks[21], (N_B_LAYERS, HEAD_DIM), 0.1),
        'attn_lambda_q2': nrm(ks[22], (N_B_LAYERS, HEAD_DIM), 0.1),
        'attn_lambda_k2': nrm(ks[23], (N_B_LAYERS, HEAD_DIM), 0.1),
        'attn_subln_g': 1.0 + nrm(ks[24], (N_B_LAYERS, V_HEAD_DIM), 0.02),
        'attn_w_o': nrm(ks[25], (N_B_LAYERS, N_HEADS * V_HEAD_DIM, D_MODEL), (N_HEADS * V_HEAD_DIM) ** -0.5),
        'ffn_w_gate': nrm(ks[26], (DEPTH, D_MODEL, D_FF), d_inv),
        'ffn_w_up': nrm(ks[27], (DEPTH, D_MODEL, D_FF), d_inv),
        'ffn_w_down': nrm(ks[28], (DEPTH, D_FF, D_MODEL), D_FF ** -0.5),
        'final_norm_g': 1.0 + nrm(ks[29], (D_MODEL,), 0.02),
    }


def reference(x_prompt, x_sample, state_conv, cache_k, cache_v, page_table,
              attn_norm_g, ffn_norm_g, conv_w_pw1, conv_b_pw1, conv_w_dw, conv_b_dw,
              conv_ln_g, conv_ln_b, conv_w_pw2, conv_b_pw2, kv_norm_g, w_k, w_v,
              attn_w_q, attn_lambda_q1, attn_lambda_k1, attn_lambda_q2, attn_lambda_k2,
              attn_subln_g, attn_w_o, ffn_w_gate, ffn_w_up, ffn_w_down, final_norm_g):
    params = dict(attn_norm_g=attn_norm_g, ffn_norm_g=ffn_norm_g, conv_w_pw1=conv_w_pw1,
                  conv_b_pw1=conv_b_pw1, conv_w_dw=conv_w_dw, conv_b_dw=conv_b_dw,
                  conv_ln_g=conv_ln_g, conv_ln_b=conv_ln_b, conv_w_pw2=conv_w_pw2,
                  conv_b_pw2=conv_b_pw2, kv_norm_g=kv_norm_g, w_k=w_k, w_v=w_v,
                  attn_w_q=attn_w_q, attn_lambda_q1=attn_lambda_q1, attn_lambda_k1=attn_lambda_k1,
                  attn_lambda_q2=attn_lambda_q2, attn_lambda_k2=attn_lambda_k2,
                  attn_subln_g=attn_subln_g, attn_w_o=attn_w_o, ffn_w_gate=ffn_w_gate,
                  ffn_w_up=ffn_w_up, ffn_w_down=ffn_w_down, final_norm_g=final_norm_g)

    def attend_prompt(q, k, v, lam):
        B, T = q.shape[0], q.shape[1]
        nb = T // Q_BLOCK
        qb = q.reshape(B, nb, Q_BLOCK, N_HEADS, 2, HEAD_DIM).swapaxes(0, 1)
        k_pos = jnp.arange(T)

        def blk(args):
            qi, start = args
            return _diff_core(qi, k, v, start + jnp.arange(Q_BLOCK), k_pos, lam)

        out = lax.map(blk, (qb, jnp.arange(nb) * Q_BLOCK))
        return out.swapaxes(0, 1).reshape(B, T, N_HEADS, V_HEAD_DIM)

    def attend_sample(q, k, v, lam):
        n_past = page_table.shape[1] * cache_k.shape[1]
        T = q.shape[1]

        def per_seq(args):
            pt, qi, ki, vi = args
            kp = cache_k[pt].reshape(n_past, N_KV_HEADS, 2, HEAD_DIM).astype(ki.dtype)
            vp = cache_v[pt].reshape(n_past, N_KV_HEADS, V_HEAD_DIM).astype(vi.dtype)
            kk = jnp.concatenate([kp, ki], axis=0)
            vv = jnp.concatenate([vp, vi], axis=0)
            q_pos = n_past + jnp.arange(T)
            k_pos = jnp.arange(n_past + T)
            return _diff_core(qi[None], kk[None], vv[None], q_pos, k_pos, lam)[0]

        return lax.map(per_seq, (page_table, q, k, v))

    conv0 = jnp.zeros((N_A_LAYERS, x_prompt.shape[0], CONV_WIDTH - 1, D_MODEL), x_prompt.dtype)
    y_prompt, conv_prompt, k_prompt, v_prompt = _trunk(x_prompt, conv0, attend_prompt, params)
    y_sample, conv_sample, k_sample, v_sample = _trunk(x_sample, state_conv, attend_sample, params)
    return (y_prompt, y_sample, conv_prompt, conv_sample, k_prompt, v_prompt, k_sample, v_sample)
```

```python
import functools
import math

import jax
import jax.numpy as jnp
from jax import lax
from jax.experimental import pallas as pl
from jax.experimental.pallas import tpu as pltpu

N_HEADS = 8
CONV_WIDTH = 31
RMS_EPS = 1e-6
SUBLN_EPS = 1e-5
LN_EPS = 1e-5
NEG_INF = -1e30
B_LAYER_INDEX = 1

V7X_VMEM_BYTES = 64 * 1024 * 1024
V7X_SUBLANES = 8
V7X_LANES = 128
VMEM_LIMIT_BYTES = V7X_VMEM_BYTES - 8 * 1024 * 1024

HIST_ROWS = 32
CONV_ROW_CHUNK = 16
CONV_COL_CHUNK = 512

f32 = jnp.float32
bf16 = jnp.bfloat16


def _params(*sem):
    return pltpu.CompilerParams(dimension_semantics=sem, vmem_limit_bytes=VMEM_LIMIT_BYTES)


def _rms_scale(x, eps):
    return lax.rsqrt(jnp.mean(x * x, axis=-1, keepdims=True) + eps)


def _sigmoid(x):
    return 1.0 / (1.0 + jnp.exp(-x))


def _dot(a, b):
    return jnp.dot(a, b, preferred_element_type=f32)


def _pw1_glu_kernel(x_ref, g_ref, wa_ref, wb_ref, ba_ref, bb_ref, u_ref, h_ref):
    @pl.when(pl.program_id(1) == 0)
    def _():
        x = x_ref[...]
        h_ref[...] = (x * _rms_scale(x, RMS_EPS) * g_ref[...]).astype(bf16)

    h = h_ref[...]
    a = _dot(h, wa_ref[...]) + ba_ref[...]
    b = _dot(h, wb_ref[...]) + bb_ref[...]
    u_ref[...] = a * _sigmoid(b)


def _pw1_glu(x, g, w, b, tm, tn):
    m, d = x.shape
    nj = d // tn
    return pl.pallas_call(
        _pw1_glu_kernel,
        grid=(m // tm, nj),
        in_specs=[
            pl.BlockSpec((tm, d), lambda i, j: (i, 0)),
            pl.BlockSpec((1, d), lambda i, j: (0, 0)),
            pl.BlockSpec((d, tn), lambda i, j: (0, j)),
            pl.BlockSpec((d, tn), lambda i, j: (0, j + nj)),
            pl.BlockSpec((1, tn), lambda i, j: (0, j)),
            pl.BlockSpec((1, tn), lambda i, j: (0, j + nj)),
        ],
        out_specs=pl.BlockSpec((tm, tn), lambda i, j: (i, j)),
        out_shape=jax.ShapeDtypeStruct((m, d), f32),
        scratch_shapes=[pltpu.VMEM((tm, d), bf16)],
        compiler_params=_params("parallel", "arbitrary"),
        name="pw1_glu",
    )(x, g, w, w, b, b)


def _conv_pw2_kernel(u_ref, uprev_ref, st_ref, x_ref, wdw_ref, bdw_ref, lng_ref, lnb_ref,
                     w2_ref, b2_ref, o_ref, full_ref, s_ref, *, tm, rc):
    i = pl.program_id(1)
    hist = jnp.where(i == 0, st_ref[0], uprev_ref[0])
    full_ref[0:HIST_ROWS, :] = hist
    full_ref[HIST_ROWS:HIST_ROWS + tm, :] = u_ref[0]
    first_tap = HIST_ROWS - (CONV_WIDTH - 1)

    d = u_ref.shape[2]
    cc = min(CONV_COL_CHUNK, d)

    def chunk(c, carry):
        r0 = pl.multiple_of(c * rc, rc)
        pieces = []
        for c0 in range(0, d, cc):
            win = full_ref[pl.ds(r0, rc + HIST_ROWS), c0:c0 + cc]
            piece = jnp.broadcast_to(bdw_ref[:, c0:c0 + cc], (rc, cc))
            for w in range(CONV_WIDTH):
                piece = piece + win[first_tap + w:first_tap + w + rc] * wdw_ref[w:w + 1, c0:c0 + cc]
            pieces.append(piece)
        acc = jnp.concatenate(pieces, axis=1)
        mu = jnp.mean(acc, axis=-1, keepdims=True)
        cen = acc - mu
        var = jnp.mean(cen * cen, axis=-1, keepdims=True)
        y = cen * lax.rsqrt(var + LN_EPS) * lng_ref[...] + lnb_ref[...]
        s_ref[pl.ds(r0, rc), :] = (y * _sigmoid(y)).astype(bf16)
        return carry

    lax.fori_loop(0, tm // rc, chunk, 0)
    o_ref[0] = x_ref[0] + _dot(s_ref[...], w2_ref[...]) + b2_ref[...]


def _conv_pw2(u, state_padded, x, wdw, bdw, lng, lnb, w2, b2, tm):
    bsz, t, d = u.shape
    rc = min(CONV_ROW_CHUNK, tm)
    hist_blocks_per_tile = tm // HIST_ROWS
    if t >= HIST_ROWS:
        uprev_spec = pl.BlockSpec(
            (1, HIST_ROWS, d), lambda b, i: (b, jnp.maximum(i * hist_blocks_per_tile - 1, 0), 0))
        uprev = u
    else:
        uprev_spec = pl.BlockSpec((1, HIST_ROWS, d), lambda b, i: (b, 0, 0))
        uprev = state_padded
    vec = pl.BlockSpec((1, d), lambda b, i: (0, 0))
    return pl.pallas_call(
        functools.partial(_conv_pw2_kernel, tm=tm, rc=rc),
        grid=(bsz, t // tm),
        in_specs=[
            pl.BlockSpec((1, tm, d), lambda b, i: (b, i, 0)),
            uprev_spec,
            pl.BlockSpec((1, HIST_ROWS, d), lambda b, i: (b, 0, 0)),
            pl.BlockSpec((1, tm, d), lambda b, i: (b, i, 0)),
            pl.BlockSpec((CONV_WIDTH, d), lambda b, i: (0, 0)),
            vec, vec, vec,
            pl.BlockSpec((d, d), lambda b, i: (0, 0), pipeline_mode=pl.Buffered(1)),
            vec,
        ],
        out_specs=pl.BlockSpec((1, tm, d), lambda b, i: (b, i, 0)),
        out_shape=jax.ShapeDtypeStruct((bsz, t, d), f32),
        scratch_shapes=[pltpu.VMEM((HIST_ROWS + tm, d), f32), pltpu.VMEM((tm, d), bf16)],
        compiler_params=_params("parallel", "arbitrary"),
        name="conv_pw2",
    )(u, uprev, state_padded, x, wdw, bdw, lng, lnb, w2, b2)


def _ffn_kernel(x_ref, g_ref, wg_ref, wu_ref, wd_ref, fg_ref, o_ref, h_ref, *, final_norm):
    j = pl.program_id(1)

    @pl.when(j == 0)
    def _():
        x = x_ref[...]
        h_ref[...] = (x * _rms_scale(x, RMS_EPS) * g_ref[...]).astype(bf16)
        o_ref[...] = x

    h = h_ref[...]
    gate = _dot(h, wg_ref[...])
    up = _dot(h, wu_ref[...])
    act = (gate * _sigmoid(gate) * up).astype(bf16)
    o_ref[...] += _dot(act, wd_ref[...])

    if final_norm:
        @pl.when(j == pl.num_programs(1) - 1)
        def _():
            y = o_ref[...]
            o_ref[...] = y * _rms_scale(y, RMS_EPS) * fg_ref[...]


def _ffn(x, g, wg, wu, wd, final_g, tm, tf, final_norm):
    m, d = x.shape
    f = wg.shape[1]
    return pl.pallas_call(
        functools.partial(_ffn_kernel, final_norm=final_norm),
        grid=(m // tm, f // tf),
        in_specs=[
            pl.BlockSpec((tm, d), lambda i, j: (i, 0), pipeline_mode=pl.Buffered(1)),
            pl.BlockSpec((1, d), lambda i, j: (0, 0)),
            pl.BlockSpec((d, tf), lambda i, j: (0, j)),
            pl.BlockSpec((d, tf), lambda i, j: (0, j)),
            pl.BlockSpec((tf, d), lambda i, j: (j, 0)),
            pl.BlockSpec((1, d), lambda i, j: (0, 0)),
        ],
        out_specs=pl.BlockSpec((tm, d), lambda i, j: (i, 0)),
        out_shape=jax.ShapeDtypeStruct((m, d), f32),
        scratch_shapes=[pltpu.VMEM((tm, d), bf16)],
        compiler_params=_params("parallel", "arbitrary"),
        name="ffn_final" if final_norm else "ffn",
    )(x, g, wg, wu, wd, final_g)


def _qkv_kernel(x_ref, gkv_ref, gq_ref, wk_ref, wv_ref, wq_ref,
                k_ref, v_ref, q_ref, kb_ref, vb_ref, hkv_ref, hq_ref, *, q_scale):
    @pl.when(pl.program_id(1) == 0)
    def _():
        x = x_ref[...]
        xn = x * _rms_scale(x, RMS_EPS)
        hkv_ref[...] = (xn * gkv_ref[...]).astype(bf16)
        hq_ref[...] = (xn * gq_ref[...]).astype(bf16)

    hkv = hkv_ref[...]
    k = _dot(hkv, wk_ref[...])
    v = _dot(hkv, wv_ref[...])
    k_ref[...] = k
    v_ref[...] = v
    kb_ref[...] = k.astype(bf16)
    vb_ref[...] = v.astype(bf16)
    q_ref[...] = (_dot(hq_ref[...], wq_ref[...]) * q_scale).astype(bf16)


def _qkv(x, gkv, gq, wk, wv, wq, tm, tn, q_scale):
    m, d = x.shape
    wspec = pl.BlockSpec((d, tn), lambda i, j: (0, j))
    ospec = pl.BlockSpec((tm, tn), lambda i, j: (i, j))
    vec = pl.BlockSpec((1, d), lambda i, j: (0, 0))
    return pl.pallas_call(
        functools.partial(_qkv_kernel, q_scale=q_scale),
        grid=(m // tm, d // tn),
        in_specs=[pl.BlockSpec((tm, d), lambda i, j: (i, 0), pipeline_mode=pl.Buffered(1)),
                  vec, vec, wspec, wspec, wspec],
        out_specs=[ospec] * 5,
        out_shape=[jax.ShapeDtypeStruct((m, d), f32), jax.ShapeDtypeStruct((m, d), f32),
                   jax.ShapeDtypeStruct((m, d), bf16), jax.ShapeDtypeStruct((m, d), bf16),
                   jax.ShapeDtypeStruct((m, d), bf16)],
        scratch_shapes=[pltpu.VMEM((tm, d), bf16), pltpu.VMEM((tm, d), bf16)],
        compiler_params=_params("parallel", "arbitrary"),
        name="qkv_proj",
    )(x, gkv, gq, wk, wv, wq)


def _lambda(lq1_ref, lk1_ref, lq2_ref, lk2_ref, lam_init):
    e1 = jnp.exp(jnp.sum(lq1_ref[...] * lk1_ref[...], axis=-1, keepdims=True))
    e2 = jnp.exp(jnp.sum(lq2_ref[...] * lk2_ref[...], axis=-1, keepdims=True))
    return e1 - e2 + lam_init


def _lam_init():
    return 0.8 - 0.6 * math.exp(-0.3 * B_LAYER_INDEX)


def _prompt_attn_kernel(q_ref, k_ref, v_ref, lq1_ref, lk1_ref, lq2_ref, lk2_ref, sg_ref, o_ref,
                        m_ref, l_ref, acc_ref, *, tq, hd):
    i = pl.program_id(2)
    lam_init = _lam_init()
    lam = _lambda(lq1_ref, lk1_ref, lq2_ref, lk2_ref, lam_init)

    m_ref[...] = jnp.full(m_ref.shape, NEG_INF, f32)
    l_ref[...] = jnp.zeros(l_ref.shape, f32)
    acc_ref[...] = jnp.zeros(acc_ref.shape, f32)

    def block(j, masked):
        r0 = pl.multiple_of(j * tq, tq)
        kblk = k_ref[0, pl.ds(r0, tq), :]
        vblk = v_ref[0, pl.ds(r0, tq), :]
        if masked:
            row = lax.broadcasted_iota(jnp.int32, (tq, tq), 0)
            col = lax.broadcasted_iota(jnp.int32, (tq, tq), 1)
            keep = col <= row
        for c in range(2):
            qc = q_ref[0, :, c * hd:(c + 1) * hd]
            s = lax.dot_general(qc, kblk[:, c * hd:(c + 1) * hd], (((1,), (1,)), ((), ())),
                                preferred_element_type=f32)
            if masked:
                s = jnp.where(keep, s, NEG_INF)
            m_old = m_ref[c]
            m_new = jnp.maximum(m_old, jnp.max(s, axis=-1, keepdims=True))
            alpha = jnp.exp(m_old - m_new)
            p = jnp.exp(s - m_new)
            l_ref[c] = alpha * l_ref[c] + jnp.sum(p, axis=-1, keepdims=True)
            acc_ref[c] = alpha * acc_ref[c] + _dot(p.astype(bf16), vblk)
            m_ref[c] = m_new

    def body(j, carry):
        block(j, masked=False)
        return carry

    lax.fori_loop(0, i, body, 0)
    block(i, masked=True)

    o = acc_ref[0] / l_ref[0] - lam * (acc_ref[1] / l_ref[1])
    o = o * _rms_scale(o, SUBLN_EPS) * sg_ref[...] * (1.0 - lam_init)
    o_ref[0] = o.astype(bf16)


def _prompt_attn(q, k, v, lq1, lk1, lq2, lk2, sg, tq):
    bsz, t, d = q.shape
    hd = d // N_HEADS // 2
    vd = 2 * hd
    lvec = pl.BlockSpec((1, hd), lambda b, h, i: (0, 0))
    return pl.pallas_call(
        functools.partial(_prompt_attn_kernel, tq=tq, hd=hd),
        grid=(bsz, N_HEADS, t // tq),
        in_specs=[
            pl.BlockSpec((1, tq, vd), lambda b, h, i: (b, i, h)),
            pl.BlockSpec((1, t, vd), lambda b, h, i: (b, 0, h)),
            pl.BlockSpec((1, t, vd), lambda b, h, i: (b, 0, h)),
            lvec, lvec, lvec, lvec,
            pl.BlockSpec((1, vd), lambda b, h, i: (0, 0)),
        ],
        out_specs=pl.BlockSpec((1, tq, vd), lambda b, h, i: (b, i, h)),
        out_shape=jax.ShapeDtypeStruct((bsz, t, d), bf16),
        scratch_shapes=[pltpu.VMEM((2, tq, 1), f32), pltpu.VMEM((2, tq, 1), f32),
                        pltpu.VMEM((2, tq, vd), f32)],
        compiler_params=_params("parallel", "parallel", "arbitrary"),
        name="prompt_diff_attn",
    )(q, k, v, lq1, lk1, lq2, lk2, sg)


def _sample_attn_kernel(pt_ref, q_ref, kn_ref, vn_ref, lq1_ref, lk1_ref, lq2_ref, lk2_ref, sg_ref,
                        *refs, pages, tq, hd):
    k_refs = refs[:pages]
    v_refs = refs[pages:2 * pages]
    o_ref = refs[2 * pages]
    qbd_ref, m_ref, l_ref, acc_ref = refs[2 * pages + 1:]
    t = pl.program_id(1)
    d = q_ref.shape[2]
    vd = 2 * hd
    nl = 2 * N_HEADS * tq
    page = k_refs[0].shape[1]
    lam_init = _lam_init()

    @pl.when(t == 0)
    def _():
        qrep = jnp.broadcast_to(q_ref[0].astype(f32)[None], (nl // tq, tq, d)).reshape(nl, d)
        row = lax.broadcasted_iota(jnp.int32, (nl, d), 0)
        col = lax.broadcasted_iota(jnp.int32, (nl, d), 1)
        comp = row // (N_HEADS * tq)
        head = (row % (N_HEADS * tq)) // tq
        qbd = jnp.where(col // hd == head * 2 + comp, qrep, 0.0)
        qbd_ref[...] = qbd.T.astype(bf16)
        m_ref[...] = jnp.full(m_ref.shape, NEG_INF, f32)
        l_ref[...] = jnp.zeros(l_ref.shape, f32)
        acc_ref[...] = jnp.zeros(acc_ref.shape, f32)

    def update(kb, vb, keep):
        n = kb.shape[0]
        s = _dot(kb, qbd_ref[...])
        if keep is not None:
            s = jnp.where(keep, s, NEG_INF)
        m_old = m_ref[...]
        m_new = jnp.maximum(m_old, jnp.max(s, axis=0, keepdims=True))
        alpha = jnp.exp(m_old - m_new)
        p = jnp.exp(s - m_new[0:1])
        l_ref[...] = alpha * l_ref[...] + jnp.sum(p, axis=0, keepdims=True)
        m_ref[...] = m_new
        pt = jnp.concatenate([p[r:r + nl].T for r in range(0, n, nl)], axis=1).astype(bf16)
        alpha_col = jnp.broadcast_to(alpha[0:1], (nl, nl)).T
        scale = jnp.concatenate([alpha_col] * (d // nl), axis=1)
        acc_ref[...] = scale * acc_ref[...] + _dot(pt, vb)

    kb = jnp.concatenate([r[0].astype(bf16) for r in k_refs], axis=0)
    vb = jnp.concatenate([r[0].astype(bf16) for r in v_refs], axis=0)
    update(kb, vb, None)

    @pl.when(t == pl.num_programs(1) - 1)
    def _():
        pad = jnp.zeros((page - tq, d), f32)
        kn = jnp.concatenate([kn_ref[0], pad], axis=0).astype(bf16)
        vn = jnp.concatenate([vn_ref[0], pad], axis=0).astype(bf16)
        tok = lax.broadcasted_iota(jnp.int32, (page, nl), 0)
        qi = lax.broadcasted_iota(jnp.int32, (page, nl), 1) % tq
        update(kn, vn, tok <= qi)

        lam = _lambda(lq1_ref, lk1_ref, lq2_ref, lk2_ref, lam_init)
        inv_l = jnp.broadcast_to(1.0 / l_ref[0:1], (nl, nl)).T
        half = N_HEADS * tq
        for h in range(N_HEADS):
            r0 = h * tq
            o0 = acc_ref[r0:r0 + tq, h * vd:(h + 1) * vd] * inv_l[r0:r0 + tq, 0:1]
            o1 = acc_ref[half + r0:half + r0 + tq, h * vd:(h + 1) * vd] * inv_l[half + r0:half + r0 + tq, 0:1]
            o = o0 - lam * o1
            o = o * _rms_scale(o, SUBLN_EPS) * sg_ref[...] * (1.0 - lam_init)
            o_ref[0, :, h * vd:(h + 1) * vd] = o.astype(bf16)


def _sample_attn(page_table, q, k_new, v_new, cache_k, cache_v, lq1, lk1, lq2, lk2, sg, pages):
    nseq, tq, d = q.shape
    hd = d // N_HEADS // 2
    n_pages = page_table.shape[1]
    page = cache_k.shape[1]
    nl = 2 * N_HEADS * tq
    assert nl == V7X_LANES and page == nl and n_pages % pages == 0

    def page_spec(p):
        return pl.BlockSpec((1, page, d), lambda s, t, pt: (pt[s, t * pages + p], 0, 0))

    row = pl.BlockSpec((1, tq, d), lambda s, t, pt: (s, 0, 0))
    lvec = pl.BlockSpec((1, hd), lambda s, t, pt: (0, 0))
    grid_spec = pltpu.PrefetchScalarGridSpec(
        num_scalar_prefetch=1,
        grid=(nseq, n_pages // pages),
        in_specs=[row, row, row, lvec, lvec, lvec, lvec,
                  pl.BlockSpec((1, 2 * hd), lambda s, t, pt: (0, 0))]
                 + [page_spec(p) for p in range(pages)] + [page_spec(p) for p in range(pages)],
        out_specs=pl.BlockSpec((1, tq, d), lambda s, t, pt: (s, 0, 0)),
        scratch_shapes=[pltpu.VMEM((d, nl), bf16), pltpu.VMEM((V7X_SUBLANES, nl), f32),
                        pltpu.VMEM((V7X_SUBLANES, nl), f32), pltpu.VMEM((nl, d), f32)],
    )
    return pl.pallas_call(
        functools.partial(_sample_attn_kernel, pages=pages, tq=tq, hd=hd),
        grid_spec=grid_spec,
        out_shape=jax.ShapeDtypeStruct((nseq, tq, d), bf16),
        compiler_params=_params("parallel", "arbitrary"),
        name="sample_paged_diff_attn",
    )(page_table, q, k_new, v_new, lq1, lk1, lq2, lk2, sg,
      *([cache_k] * pages), *([cache_v] * pages))


def _oproj_kernel(o_ref, w_ref, x_ref, y_ref):
    y_ref[...] = x_ref[...] + _dot(o_ref[...], w_ref[...])


def _oproj(o, w, x, tm, tn):
    m, d = x.shape
    return pl.pallas_call(
        _oproj_kernel,
        grid=(m // tm, d // tn),
        in_specs=[pl.BlockSpec((tm, o.shape[1]), lambda i, j: (i, 0)),
                  pl.BlockSpec((o.shape[1], tn), lambda i, j: (0, j)),
                  pl.BlockSpec((tm, tn), lambda i, j: (i, j))],
        out_specs=pl.BlockSpec((tm, tn), lambda i, j: (i, j)),
        out_shape=jax.ShapeDtypeStruct((m, d), f32),
        compiler_params=_params("parallel", "arbitrary"),
        name="attn_out_proj",
    )(o, w, x)


def _trunk(x, state, attend, p, tm, conv_tm):
    bsz, t, d = x.shape
    m = bsz * t
    hd = d // N_HEADS // 2
    tn = min(512, d)
    x2 = x.reshape(m, d)

    u = _pw1_glu(x2, p['attn_norm_g'][0:1], p['conv_w_pw1'], p['conv_b_pw1'], tm, tn)
    u3 = u.reshape(bsz, t, d)
    state_padded = jnp.pad(state, ((0, 0), (HIST_ROWS - (CONV_WIDTH - 1), 0), (0, 0)))
    x1 = _conv_pw2(u3, state_padded, x, p['conv_w_dw'], p['conv_b_dw'], p['conv_ln_g'],
                   p['conv_ln_b'], p['conv_w_pw2'], p['conv_b_pw2'], conv_tm)
    new_state = jnp.concatenate([state, u3], axis=1)[:, t:]

    x1 = _ffn(x1.reshape(m, d), p['ffn_norm_g'][0:1], p['ffn_w_gate'][0], p['ffn_w_up'][0],
              p['ffn_w_down'][0], p['final_norm_g'], tm, 512, final_norm=False)

    k, v, q, kb, vb = _qkv(x1, p['kv_norm_g'], p['attn_norm_g'][1:2], p['w_k'], p['w_v'],
                           p['attn_w_q'], tm, tn // 2, hd ** -0.5)
    o = attend(q.reshape(bsz, t, d), k.reshape(bsz, t, d), v.reshape(bsz, t, d),
               kb.reshape(bsz, t, d), vb.reshape(bsz, t, d))
    x2 = _oproj(o.reshape(m, d), p['attn_w_o'], x1, tm, tn)
    y = _ffn(x2, p['ffn_norm_g'][1:2], p['ffn_w_gate'][1], p['ffn_w_up'][1],
             p['ffn_w_down'][1], p['final_norm_g'], tm, 512, final_norm=True)
    return (y.reshape(bsz, t, d), new_state[None],
            k.reshape(bsz, t, 2 * N_HEADS, hd), v.reshape(bsz, t, N_HEADS, 2 * hd))


def kernel(x_prompt, x_sample, state_conv, cache_k, cache_v, page_table, attn_norm_g, ffn_norm_g,
           conv_w_pw1, conv_b_pw1, conv_w_dw, conv_b_dw, conv_ln_g, conv_ln_b, conv_w_pw2,
           conv_b_pw2, kv_norm_g, w_k, w_v, attn_w_q, attn_lambda_q1, attn_lambda_k1,
           attn_lambda_q2, attn_lambda_k2, attn_subln_g, attn_w_o, ffn_w_gate, ffn_w_up,
           ffn_w_down, final_norm_g):
    assert attn_norm_g.shape[0] == 2 and conv_w_pw1.shape[0] == 1 and attn_w_q.shape[0] == 1
    d = x_prompt.shape[-1]
    p = dict(
        attn_norm_g=attn_norm_g, ffn_norm_g=ffn_norm_g,
        conv_w_pw1=conv_w_pw1[0].astype(bf16), conv_b_pw1=conv_b_pw1,
        conv_w_dw=conv_w_dw[0], conv_b_dw=conv_b_dw, conv_ln_g=conv_ln_g, conv_ln_b=conv_ln_b,
        conv_w_pw2=conv_w_pw2[0].astype(bf16), conv_b_pw2=conv_b_pw2,
        kv_norm_g=kv_norm_g.reshape(1, d), w_k=w_k.astype(bf16), w_v=w_v.astype(bf16),
        attn_w_q=attn_w_q[0].astype(bf16), attn_w_o=attn_w_o[0].astype(bf16),
        ffn_w_gate=ffn_w_gate.astype(bf16), ffn_w_up=ffn_w_up.astype(bf16),
        ffn_w_down=ffn_w_down.astype(bf16), final_norm_g=final_norm_g.reshape(1, d),
    )
    lams = (attn_lambda_q1, attn_lambda_k1, attn_lambda_q2, attn_lambda_k2, attn_subln_g)

    def attend_prompt(q, k, v, kb, vb):
        return _prompt_attn(q, kb, vb, *lams, tq=256)

    ck = cache_k.reshape(cache_k.shape[0], cache_k.shape[1], d)
    cv = cache_v.reshape(cache_v.shape[0], cache_v.shape[1], d)

    def attend_sample(q, k, v, kb, vb):
        return _sample_attn(page_table, q, k, v, ck, cv, *lams, pages=4)

    bsz, t, _ = x_prompt.shape
    conv0 = jnp.zeros((bsz, CONV_WIDTH - 1, d), x_prompt.dtype)
    y_p, conv_p, k_p, v_p = _trunk(x_prompt, conv0, attend_prompt, p, tm=1024, conv_tm=512)
    sb, st, _ = x_sample.shape
    y_s, conv_s, k_s, v_s = _trunk(x_sample, state_conv[0], attend_sample, p, tm=sb * st, conv_tm=st)
    return (y_p, y_s, conv_p, conv_s, k_p, v_p, k_s, v_s)
```

```python
import functools
import math

import jax
import jax.numpy as jnp
from jax import lax
from jax.experimental import pallas as pl
from jax.experimental.pallas import tpu as pltpu

N_HEADS = 8
CONV_WIDTH = 31
RMS_EPS = 1e-6
SUBLN_EPS = 1e-5
LN_EPS = 1e-5
NEG_INF = -1e30
B_LAYER_INDEX = 1

V7X_VMEM_BYTES = 64 * 1024 * 1024
V7X_SUBLANES = 8
V7X_LANES = 128
VMEM_LIMIT_BYTES = V7X_VMEM_BYTES - 8 * 1024 * 1024

HIST_ROWS = 32
CONV_ROW_CHUNK = 32
CONV_COL_CHUNK = 256
ATTN_KEY_SPLIT = 2

f32 = jnp.float32
bf16 = jnp.bfloat16


def _params(*sem):
    return pltpu.CompilerParams(dimension_semantics=sem, vmem_limit_bytes=VMEM_LIMIT_BYTES)


def _rms_scale(x, eps):
    return lax.rsqrt(jnp.mean(x * x, axis=-1, keepdims=True) + eps)


def _sigmoid(x):
    return 1.0 / (1.0 + jnp.exp(-x))


def _dot(a, b):
    return jnp.dot(a, b, preferred_element_type=f32)


def _pw1_glu_kernel(x_ref, g_ref, wa_ref, wb_ref, ba_ref, bb_ref, u_ref, h_ref):
    @pl.when(pl.program_id(1) == 0)
    def _():
        x = x_ref[...]
        h_ref[...] = (x * _rms_scale(x, RMS_EPS) * g_ref[...]).astype(bf16)

    h = h_ref[...]
    a = _dot(h, wa_ref[...]) + ba_ref[...]
    b = _dot(h, wb_ref[...]) + bb_ref[...]
    u_ref[...] = a * _sigmoid(b)


def _pw1_glu(x, g, w, b, tm, tn):
    m, d = x.shape
    nj = d // tn
    return pl.pallas_call(
        _pw1_glu_kernel,
        grid=(m // tm, nj),
        in_specs=[
            pl.BlockSpec((tm, d), lambda i, j: (i, 0)),
            pl.BlockSpec((1, d), lambda i, j: (0, 0)),
            pl.BlockSpec((d, tn), lambda i, j: (0, j)),
            pl.BlockSpec((d, tn), lambda i, j: (0, j + nj)),
            pl.BlockSpec((1, tn), lambda i, j: (0, j)),
            pl.BlockSpec((1, tn), lambda i, j: (0, j + nj)),
        ],
        out_specs=pl.BlockSpec((tm, tn), lambda i, j: (i, j)),
        out_shape=jax.ShapeDtypeStruct((m, d), f32),
        scratch_shapes=[pltpu.VMEM((tm, d), bf16)],
        compiler_params=_params("parallel", "arbitrary"),
        name="pw1_glu",
    )(x, g, w, w, b, b)


def _conv_pw2_kernel(u_ref, uprev_ref, st_ref, x_ref, wdw_ref, bdw_ref, lng_ref, lnb_ref,
                     w2_ref, b2_ref, o_ref, full_ref, s_ref, *, tm, rc):
    i = pl.program_id(1)
    hist = jnp.where(i == 0, st_ref[0], uprev_ref[0])
    full_ref[0:HIST_ROWS, :] = hist
    full_ref[HIST_ROWS:HIST_ROWS + tm, :] = u_ref[0]
    first_tap = HIST_ROWS - (CONV_WIDTH - 1)

    d = u_ref.shape[2]
    cc = min(CONV_COL_CHUNK, d)

    taps_by_shift = [[(a, 8 * a + r - first_tap) for a in range(HIST_ROWS // 8 + 1)
                      if 0 <= 8 * a + r - first_tap < CONV_WIDTH] for r in range(8)]

    def chunk(c, carry):
        r0 = pl.multiple_of(c * rc, rc)
        pieces = []
        for c0 in range(0, d, cc):
            piece = jnp.broadcast_to(bdw_ref[:, c0:c0 + cc], (rc, cc))
            for r, taps in enumerate(taps_by_shift):
                rows = rc + (8 if r else 0)
                g = None
                for a, w in taps:
                    win = full_ref[pl.ds(pl.multiple_of(r0 + 8 * a, 8), rows), c0:c0 + cc]
                    term = win.reshape(rows // 8, 8, cc) * wdw_ref[w, :, c0:c0 + cc][None]
                    g = term if g is None else g + term
                piece = piece + g.reshape(rows, cc)[r:r + rc]
            pieces.append(piece)
        acc = jnp.concatenate(pieces, axis=1)
        mu = jnp.mean(acc, axis=-1, keepdims=True)
        cen = acc - mu
        var = jnp.mean(cen * cen, axis=-1, keepdims=True)
        y = cen * lax.rsqrt(var + LN_EPS) * lng_ref[...] + lnb_ref[...]
        s_ref[pl.ds(r0, rc), :] = (y * _sigmoid(y)).astype(bf16)
        return carry

    lax.fori_loop(0, tm // rc, chunk, 0)
    o_ref[0] = x_ref[0] + _dot(s_ref[...], w2_ref[...]) + b2_ref[...]


def _conv_pw2(u, state_padded, x, wdw, bdw, lng, lnb, w2, b2, tm):
    bsz, t, d = u.shape
    rc = min(CONV_ROW_CHUNK, tm)
    hist_blocks_per_tile = tm // HIST_ROWS
    if t >= HIST_ROWS:
        uprev_spec = pl.BlockSpec(
            (1, HIST_ROWS, d), lambda b, i: (b, jnp.maximum(i * hist_blocks_per_tile - 1, 0), 0))
        uprev = u
    else:
        uprev_spec = pl.BlockSpec((1, HIST_ROWS, d), lambda b, i: (b, 0, 0))
        uprev = state_padded
    vec = pl.BlockSpec((1, d), lambda b, i: (0, 0))
    return pl.pallas_call(
        functools.partial(_conv_pw2_kernel, tm=tm, rc=rc),
        grid=(bsz, t // tm),
        in_specs=[
            pl.BlockSpec((1, tm, d), lambda b, i: (b, i, 0)),
            uprev_spec,
            pl.BlockSpec((1, HIST_ROWS, d), lambda b, i: (b, 0, 0)),
            pl.BlockSpec((1, tm, d), lambda b, i: (b, i, 0)),
            pl.BlockSpec((CONV_WIDTH, V7X_SUBLANES, d), lambda b, i: (0, 0, 0)),
            vec, vec, vec,
            pl.BlockSpec((d, d), lambda b, i: (0, 0), pipeline_mode=pl.Buffered(1)),
            vec,
        ],
        out_specs=pl.BlockSpec((1, tm, d), lambda b, i: (b, i, 0)),
        out_shape=jax.ShapeDtypeStruct((bsz, t, d), f32),
        scratch_shapes=[pltpu.VMEM((HIST_ROWS + tm, d), f32), pltpu.VMEM((tm, d), bf16)],
        compiler_params=_params("parallel", "arbitrary"),
        name="conv_pw2",
    )(u, uprev, state_padded, x, wdw, bdw, lng, lnb, w2, b2)


def _ffn_kernel(x_ref, g_ref, wg_ref, wu_ref, wd_ref, fg_ref, o_ref, h_ref, *, final_norm):
    j = pl.program_id(1)

    @pl.when(j == 0)
    def _():
        x = x_ref[...]
        h_ref[...] = (x * _rms_scale(x, RMS_EPS) * g_ref[...]).astype(bf16)
        o_ref[...] = x

    h = h_ref[...]
    gate = _dot(h, wg_ref[...])
    up = _dot(h, wu_ref[...])
    act = (gate * _sigmoid(gate) * up).astype(bf16)
    o_ref[...] += _dot(act, wd_ref[...])

    if final_norm:
        @pl.when(j == pl.num_programs(1) - 1)
        def _():
            y = o_ref[...]
            o_ref[...] = y * _rms_scale(y, RMS_EPS) * fg_ref[...]


def _ffn(x, g, wg, wu, wd, final_g, tm, tf, final_norm):
    m, d = x.shape
    f = wg.shape[1]
    return pl.pallas_call(
        functools.partial(_ffn_kernel, final_norm=final_norm),
        grid=(m // tm, f // tf),
        in_specs=[
            pl.BlockSpec((tm, d), lambda i, j: (i, 0), pipeline_mode=pl.Buffered(1)),
            pl.BlockSpec((1, d), lambda i, j: (0, 0)),
            pl.BlockSpec((d, tf), lambda i, j: (0, j)),
            pl.BlockSpec((d, tf), lambda i, j: (0, j)),
            pl.BlockSpec((tf, d), lambda i, j: (j, 0)),
            pl.BlockSpec((1, d), lambda i, j: (0, 0)),
        ],
        out_specs=pl.BlockSpec((tm, d), lambda i, j: (i, 0)),
        out_shape=jax.ShapeDtypeStruct((m, d), f32),
        scratch_shapes=[pltpu.VMEM((tm, d), bf16)],
        compiler_params=_params("parallel", "arbitrary"),
        name="ffn_final" if final_norm else "ffn",
    )(x, g, wg, wu, wd, final_g)


def _qkv_kernel(x_ref, gkv_ref, gq_ref, wk_ref, wv_ref, wq_ref,
                k_ref, v_ref, q_ref, kb_ref, vb_ref, hkv_ref, hq_ref, *, q_scale):
    @pl.when(pl.program_id(1) == 0)
    def _():
        x = x_ref[...]
        xn = x * _rms_scale(x, RMS_EPS)
        hkv_ref[...] = (xn * gkv_ref[...]).astype(bf16)
        hq_ref[...] = (xn * gq_ref[...]).astype(bf16)

    hkv = hkv_ref[...]
    k = _dot(hkv, wk_ref[...])
    v = _dot(hkv, wv_ref[...])
    k_ref[...] = k
    v_ref[...] = v
    kb_ref[...] = k.astype(bf16)
    vb_ref[...] = v.astype(bf16)
    q_ref[...] = (_dot(hq_ref[...], wq_ref[...]) * q_scale).astype(bf16)


def _qkv(x, gkv, gq, wk, wv, wq, tm, tn, q_scale):
    m, d = x.shape
    wspec = pl.BlockSpec((d, tn), lambda i, j: (0, j))
    ospec = pl.BlockSpec((tm, tn), lambda i, j: (i, j))
    vec = pl.BlockSpec((1, d), lambda i, j: (0, 0))
    return pl.pallas_call(
        functools.partial(_qkv_kernel, q_scale=q_scale),
        grid=(m // tm, d // tn),
        in_specs=[pl.BlockSpec((tm, d), lambda i, j: (i, 0), pipeline_mode=pl.Buffered(1)),
                  vec, vec, wspec, wspec, wspec],
        out_specs=[ospec] * 5,
        out_shape=[jax.ShapeDtypeStruct((m, d), f32), jax.ShapeDtypeStruct((m, d), f32),
                   jax.ShapeDtypeStruct((m, d), bf16), jax.ShapeDtypeStruct((m, d), bf16),
                   jax.ShapeDtypeStruct((m, d), bf16)],
        scratch_shapes=[pltpu.VMEM((tm, d), bf16), pltpu.VMEM((tm, d), bf16)],
        compiler_params=_params("parallel", "arbitrary"),
        name="qkv_proj",
    )(x, gkv, gq, wk, wv, wq)


def _lambda(lq1_ref, lk1_ref, lq2_ref, lk2_ref, lam_init):
    e1 = jnp.exp(jnp.sum(lq1_ref[...] * lk1_ref[...], axis=-1, keepdims=True))
    e2 = jnp.exp(jnp.sum(lq2_ref[...] * lk2_ref[...], axis=-1, keepdims=True))
    return e1 - e2 + lam_init


def _lam_init():
    return 0.8 - 0.6 * math.exp(-0.3 * B_LAYER_INDEX)


def _prompt_attn_kernel(q_ref, k_ref, v_ref, lq1_ref, lk1_ref, lq2_ref, lk2_ref, sg_ref, o_ref,
                        vt_ref, qt_ref, m_ref, l_ref, acc_ref, *, tq, hd):
    i = pl.program_id(2)
    t = k_ref.shape[1]
    lam_init = _lam_init()
    lam = _lambda(lq1_ref, lk1_ref, lq2_ref, lk2_ref, lam_init)

    @pl.when(i == 0)
    def _():
        for r0 in range(0, t, tq):
            vt_ref[:, r0:r0 + tq] = v_ref[0, r0:r0 + tq, :].astype(f32).T.astype(bf16)

    for c in range(2):
        qt_ref[c] = q_ref[0, :, c * hd:(c + 1) * hd].astype(f32).T.astype(bf16)
    m_ref[...] = jnp.full(m_ref.shape, NEG_INF, f32)
    l_ref[...] = jnp.zeros(l_ref.shape, f32)
    acc_ref[...] = jnp.zeros(acc_ref.shape, f32)

    def block(j, masked):
        r0 = pl.multiple_of(j * tq, tq)
        tk = tq // ATTN_KEY_SPLIT
        kblk = k_ref[0, pl.ds(r0, tq), :]
        vtblk = vt_ref[:, pl.ds(r0, tq)]
        s = [[_dot(kblk[a * tk:(a + 1) * tk, c * hd:(c + 1) * hd], qt_ref[c]) for c in range(2)]
             for a in range(ATTN_KEY_SPLIT)]
        for a in range(ATTN_KEY_SPLIT):
            if masked:
                key = lax.broadcasted_iota(jnp.int32, (tk, tq), 0) + a * tk
                query = lax.broadcasted_iota(jnp.int32, (tk, tq), 1)
                keep = key <= query
            for c in range(2):
                sc = jnp.where(keep, s[a][c], NEG_INF) if masked else s[a][c]
                m_old = m_ref[c]
                m_new = jnp.maximum(m_old, jnp.max(sc, axis=0, keepdims=True))
                alpha = jnp.exp2(m_old - m_new)
                p = jnp.exp2(sc - m_new[0:1])
                l_ref[c] = alpha * l_ref[c] + jnp.sum(p, axis=0, keepdims=True)
                acc_ref[c] = alpha[0:1] * acc_ref[c] + _dot(vtblk[:, a * tk:(a + 1) * tk], p.astype(bf16))
                m_ref[c] = m_new

    def body(j, carry):
        block(j, masked=False)
        return carry

    lax.fori_loop(0, i, body, 0)
    block(i, masked=True)

    ot = acc_ref[0] * (1.0 / l_ref[0][0:1]) - lam * (acc_ref[1] * (1.0 / l_ref[1][0:1]))
    o = ot.T
    o = o * _rms_scale(o, SUBLN_EPS) * sg_ref[...] * (1.0 - lam_init)
    o_ref[0] = o.astype(bf16)


def _prompt_attn(q, k, v, lq1, lk1, lq2, lk2, sg, tq):
    bsz, t, d = q.shape
    hd = d // N_HEADS // 2
    vd = 2 * hd
    lvec = pl.BlockSpec((1, hd), lambda b, h, i: (0, 0))
    return pl.pallas_call(
        functools.partial(_prompt_attn_kernel, tq=tq, hd=hd),
        grid=(bsz, N_HEADS, t // tq),
        in_specs=[
            pl.BlockSpec((1, tq, vd), lambda b, h, i: (b, i, h)),
            pl.BlockSpec((1, t, vd), lambda b, h, i: (b, 0, h)),
            pl.BlockSpec((1, t, vd), lambda b, h, i: (b, 0, h)),
            lvec, lvec, lvec, lvec,
            pl.BlockSpec((1, vd), lambda b, h, i: (0, 0)),
        ],
        out_specs=pl.BlockSpec((1, tq, vd), lambda b, h, i: (b, i, h)),
        out_shape=jax.ShapeDtypeStruct((bsz, t, d), bf16),
        scratch_shapes=[pltpu.VMEM((vd, t), bf16), pltpu.VMEM((2, hd, tq), bf16),
                        pltpu.VMEM((2, V7X_SUBLANES, tq), f32), pltpu.VMEM((2, V7X_SUBLANES, tq), f32),
                        pltpu.VMEM((2, vd, tq), f32)],
        compiler_params=_params("parallel", "parallel", "arbitrary"),
        name="prompt_diff_attn",
    )(q, k, v, lq1, lk1, lq2, lk2, sg)


def _sample_attn_kernel(pt_ref, q_ref, kn_ref, vn_ref, lq1_ref, lk1_ref, lq2_ref, lk2_ref, sg_ref,
                        *refs, pages, tq, hd):
    k_refs = refs[:pages]
    v_refs = refs[pages:2 * pages]
    o_ref = refs[2 * pages]
    qmt_ref, bias_ref, m_ref, l_ref, acc_ref = refs[2 * pages + 1:]
    t = pl.program_id(1)
    vd = 2 * hd
    nl = 2 * N_HEADS * tq
    half = N_HEADS * tq
    page = k_refs[0].shape[1]
    lam_init = _lam_init()

    @pl.when(t == 0)
    def _():
        qf = q_ref[0].astype(f32)
        blocks = [qf[:, (2 * h + c) * hd:(2 * h + c + 1) * hd] for c in range(2) for h in range(N_HEADS)]
        qmt_ref[...] = jnp.concatenate(blocks, axis=0).T
        comp = lax.broadcasted_iota(jnp.int32, (2, N_HEADS, nl), 0)
        head = lax.broadcasted_iota(jnp.int32, (2, N_HEADS, nl), 1)
        lane = lax.broadcasted_iota(jnp.int32, (2, N_HEADS, nl), 2)
        own = (lane // half == comp) & ((lane % half) // tq == head)
        bias_ref[...] = jnp.where(own, 0.0, NEG_INF)
        m_ref[...] = jnp.full(m_ref.shape, NEG_INF, f32)
        l_ref[...] = jnp.zeros(l_ref.shape, f32)
        acc_ref[...] = jnp.zeros(acc_ref.shape, f32)

    def update(k_parts, v2, causal_bias):
        n = v2.shape[0]
        s = []
        for c in range(2):
            sc = _dot(k_parts[c], qmt_ref[...]).reshape(n // N_HEADS, N_HEADS, nl) + bias_ref[c][None]
            if causal_bias is not None:
                sc = sc + causal_bias
            s.append(sc)
        m_old = m_ref[...]
        smax = jnp.maximum(jnp.max(s[0], axis=0), jnp.max(s[1], axis=0))
        m_new = jnp.maximum(m_old, jnp.max(smax, axis=0, keepdims=True))
        alpha = jnp.exp2(m_old - m_new)
        p = jnp.exp2(s[0] - m_new[None]) + jnp.exp2(s[1] - m_new[None])
        l_ref[...] = alpha * l_ref[...] + jnp.sum(jnp.sum(p, axis=0), axis=0, keepdims=True)
        m_ref[...] = m_new
        pv = lax.dot_general(p.reshape(n, nl), v2, (((0,), (0,)), ((), ())),
                             preferred_element_type=f32)
        alpha_col = jnp.broadcast_to(alpha[0:1], (nl, nl)).T
        acc_ref[...] = jnp.concatenate([alpha_col] * (vd // nl), axis=1) * acc_ref[...] + pv

    k_parts = [jnp.concatenate([r[0, :, pl.ds(c, N_HEADS, stride=2), :].reshape(page * N_HEADS, hd)
                                for r in k_refs], axis=0) for c in range(2)]
    v2 = jnp.concatenate([r[0].reshape(page * N_HEADS, vd) for r in v_refs], axis=0)
    update(k_parts, v2, None)

    @pl.when(t == pl.num_programs(1) - 1)
    def _():
        kn = [kn_ref[0, pl.ds(c, tq * N_HEADS, stride=2), :] for c in range(2)]
        tok = lax.broadcasted_iota(jnp.int32, (tq, N_HEADS, nl), 0)
        qi = lax.broadcasted_iota(jnp.int32, (tq, N_HEADS, nl), 2) % tq
        update(kn, vn_ref[0], jnp.where(tok <= qi, 0.0, NEG_INF))

        lam = _lambda(lq1_ref, lk1_ref, lq2_ref, lk2_ref, lam_init)
        inv_l = jnp.broadcast_to(1.0 / l_ref[0:1], (nl, nl)).T
        o = acc_ref[0:half] * inv_l[0:half, 0:1] - lam * (acc_ref[half:nl] * inv_l[half:nl, 0:1])
        o = o * _rms_scale(o, SUBLN_EPS) * sg_ref[...] * (1.0 - lam_init)
        for h in range(N_HEADS):
            o_ref[0, :, h * vd:(h + 1) * vd] = o[h * tq:(h + 1) * tq].astype(bf16)


def _sample_attn(page_table, q, k_new, v_new, cache_k, cache_v, lq1, lk1, lq2, lk2, sg, pages):
    nseq, tq, d = q.shape
    hd = d // N_HEADS // 2
    n_pages = page_table.shape[1]
    page = cache_k.shape[1]
    nl = 2 * N_HEADS * tq
    assert nl == V7X_LANES and hd == V7X_LANES and n_pages % pages == 0

    def k_spec(p):
        return pl.BlockSpec((1, page, 2 * N_HEADS, hd), lambda s, t, pt: (pt[s, t * pages + p], 0, 0, 0))

    def v_spec(p):
        return pl.BlockSpec((1, page, N_HEADS, 2 * hd), lambda s, t, pt: (pt[s, t * pages + p], 0, 0, 0))

    lvec = pl.BlockSpec((1, hd), lambda s, t, pt: (0, 0))
    grid_spec = pltpu.PrefetchScalarGridSpec(
        num_scalar_prefetch=1,
        grid=(nseq, n_pages // pages),
        in_specs=[pl.BlockSpec((1, tq, d), lambda s, t, pt: (s, 0, 0)),
                  pl.BlockSpec((1, tq * 2 * N_HEADS, hd), lambda s, t, pt: (s, 0, 0)),
                  pl.BlockSpec((1, tq * N_HEADS, 2 * hd), lambda s, t, pt: (s, 0, 0)),
                  lvec, lvec, lvec, lvec,
                  pl.BlockSpec((1, 2 * hd), lambda s, t, pt: (0, 0))]
                 + [k_spec(p) for p in range(pages)] + [v_spec(p) for p in range(pages)],
        out_specs=pl.BlockSpec((1, tq, d), lambda s, t, pt: (s, 0, 0)),
        scratch_shapes=[pltpu.VMEM((hd, nl), f32), pltpu.VMEM((2, N_HEADS, nl), f32),
                        pltpu.VMEM((V7X_SUBLANES, nl), f32), pltpu.VMEM((V7X_SUBLANES, nl), f32),
                        pltpu.VMEM((nl, 2 * hd), f32)],
    )
    return pl.pallas_call(
        functools.partial(_sample_attn_kernel, pages=pages, tq=tq, hd=hd),
        grid_spec=grid_spec,
        out_shape=jax.ShapeDtypeStruct((nseq, tq, d), bf16),
        compiler_params=_params("parallel", "arbitrary"),
        name="sample_paged_diff_attn",
    )(page_table, q, k_new, v_new, lq1, lk1, lq2, lk2, sg,
      *([cache_k] * pages), *([cache_v] * pages))


def _oproj_kernel(o_ref, w_ref, x_ref, y_ref):
    y_ref[...] = x_ref[...] + _dot(o_ref[...], w_ref[...])


def _oproj(o, w, x, tm, tn):
    m, d = x.shape
    return pl.pallas_call(
        _oproj_kernel,
        grid=(m // tm, d // tn),
        in_specs=[pl.BlockSpec((tm, o.shape[1]), lambda i, j: (i, 0)),
                  pl.BlockSpec((o.shape[1], tn), lambda i, j: (0, j)),
                  pl.BlockSpec((tm, tn), lambda i, j: (i, j))],
        out_specs=pl.BlockSpec((tm, tn), lambda i, j: (i, j)),
        out_shape=jax.ShapeDtypeStruct((m, d), f32),
        compiler_params=_params("parallel", "arbitrary"),
        name="attn_out_proj",
    )(o, w, x)


def _trunk(x, state, attend, p, tm, conv_tm):
    bsz, t, d = x.shape
    m = bsz * t
    hd = d // N_HEADS // 2
    tn = min(512, d)
    x2 = x.reshape(m, d)

    u = _pw1_glu(x2, p['attn_norm_g'][0:1], p['conv_w_pw1'], p['conv_b_pw1'], tm, tn)
    u3 = u.reshape(bsz, t, d)
    state_padded = jnp.pad(state, ((0, 0), (HIST_ROWS - (CONV_WIDTH - 1), 0), (0, 0)))
    x1 = _conv_pw2(u3, state_padded, x, p['conv_w_dw'], p['conv_b_dw'], p['conv_ln_g'],
                   p['conv_ln_b'], p['conv_w_pw2'], p['conv_b_pw2'], conv_tm)
    new_state = jnp.concatenate([state, u3], axis=1)[:, t:]

    x1 = _ffn(x1.reshape(m, d), p['ffn_norm_g'][0:1], p['ffn_w_gate'][0], p['ffn_w_up'][0],
              p['ffn_w_down'][0], p['final_norm_g'], tm, 512, final_norm=False)

    k, v, q, kb, vb = _qkv(x1, p['kv_norm_g'], p['attn_norm_g'][1:2], p['w_k'], p['w_v'],
                           p['attn_w_q'], tm, tn // 2, hd ** -0.5 * math.log2(math.e))
    o = attend(q.reshape(bsz, t, d), k.reshape(bsz, t, d), v.reshape(bsz, t, d),
               kb.reshape(bsz, t, d), vb.reshape(bsz, t, d))
    x2 = _oproj(o.reshape(m, d), p['attn_w_o'], x1, tm, tn)
    y = _ffn(x2, p['ffn_norm_g'][1:2], p['ffn_w_gate'][1], p['ffn_w_up'][1],
             p['ffn_w_down'][1], p['final_norm_g'], tm, 512, final_norm=True)
    return (y.reshape(bsz, t, d), new_state[None],
            k.reshape(bsz, t, 2 * N_HEADS, hd), v.reshape(bsz, t, N_HEADS, 2 * hd))


def kernel(x_prompt, x_sample, state_conv, cache_k, cache_v, page_table, attn_norm_g, ffn_norm_g,
           conv_w_pw1, conv_b_pw1, conv_w_dw, conv_b_dw, conv_ln_g, conv_ln_b, conv_w_pw2,
           conv_b_pw2, kv_norm_g, w_k, w_v, attn_w_q, attn_lambda_q1, attn_lambda_k1,
           attn_lambda_q2, attn_lambda_k2, attn_subln_g, attn_w_o, ffn_w_gate, ffn_w_up,
           ffn_w_down, final_norm_g):
    assert attn_norm_g.shape[0] == 2 and conv_w_pw1.shape[0] == 1 and attn_w_q.shape[0] == 1
    d = x_prompt.shape[-1]
    p = dict(
        attn_norm_g=attn_norm_g, ffn_norm_g=ffn_norm_g,
        conv_w_pw1=conv_w_pw1[0].astype(bf16), conv_b_pw1=conv_b_pw1,
        conv_w_dw=jnp.broadcast_to(conv_w_dw[0][:, None, :], (CONV_WIDTH, V7X_SUBLANES, d)),
        conv_b_dw=conv_b_dw, conv_ln_g=conv_ln_g, conv_ln_b=conv_ln_b,
        conv_w_pw2=conv_w_pw2[0].astype(bf16), conv_b_pw2=conv_b_pw2,
        kv_norm_g=kv_norm_g.reshape(1, d), w_k=w_k.astype(bf16), w_v=w_v.astype(bf16),
        attn_w_q=attn_w_q[0].astype(bf16), attn_w_o=attn_w_o[0].astype(bf16),
        ffn_w_gate=ffn_w_gate.astype(bf16), ffn_w_up=ffn_w_up.astype(bf16),
        ffn_w_down=ffn_w_down.astype(bf16), final_norm_g=final_norm_g.reshape(1, d),
    )
    lams = (attn_lambda_q1, attn_lambda_k1, attn_lambda_q2, attn_lambda_k2, attn_subln_g)

    def attend_prompt(q, k, v, kb, vb):
        return _prompt_attn(q, kb, vb, *lams, tq=512)

    def attend_sample(q, k, v, kb, vb):
        s, tq, _ = q.shape
        hd = d // N_HEADS // 2
        k_rows = k.reshape(s, tq * 2 * N_HEADS, hd)
        v_rows = v.reshape(s, tq * N_HEADS, 2 * hd)
        return _sample_attn(page_table, q, k_rows, v_rows, cache_k, cache_v, *lams, pages=4)

    bsz, t, _ = x_prompt.shape
    conv0 = jnp.zeros((bsz, CONV_WIDTH - 1, d), x_prompt.dtype)
    y_p, conv_p, k_p, v_p = _trunk(x_prompt, conv0, attend_prompt, p, tm=1024, conv_tm=512)
    sb, st, _ = x_sample.shape
    y_s, conv_s, k_s, v_s = _trunk(x_sample, state_conv[0], attend_sample, p, tm=sb * st, conv_tm=st)
    return (y_p, y_s, conv_p, conv_s, k_p, v_p, k_s, v_s)
```

```python
import functools
import math

import jax
import jax.numpy as jnp
from jax import lax
from jax.experimental import pallas as pl
from jax.experimental.pallas import tpu as pltpu

N_HEADS = 8
CONV_WIDTH = 31
RMS_EPS = 1e-6
SUBLN_EPS = 1e-5
LN_EPS = 1e-5
NEG_INF = -1e30
B_LAYER_INDEX = 1

V7X_VMEM_BYTES = 64 * 1024 * 1024
V7X_SUBLANES = 8
V7X_LANES = 128
VMEM_LIMIT_BYTES = V7X_VMEM_BYTES - 8 * 1024 * 1024

HIST_ROWS = 32
CONV_ROW_CHUNK = 32
CONV_COL_CHUNK = 256
CONV_GROUP_ROWS = 128
CONV_MAX_ROWS = 1024
ATTN_KEY_SPLIT = 2
FFN_TILE = 512

f32 = jnp.float32
bf16 = jnp.bfloat16


def _params(*sem):
    return pltpu.CompilerParams(dimension_semantics=sem, vmem_limit_bytes=VMEM_LIMIT_BYTES)


def _rms_scale(x, eps):
    return lax.rsqrt(jnp.mean(x * x, axis=-1, keepdims=True) + eps)


def _sigmoid(x):
    return 1.0 / (1.0 + jnp.exp(-x))


def _aligned(i, n):
    return i if isinstance(i, int) else pl.multiple_of(i, n)


def _dot(a, b):
    return jnp.dot(a, b, preferred_element_type=f32)


def _pw1_glu_kernel(x_ref, g_ref, wa_ref, wb_ref, ba_ref, bb_ref, u_ref, h_ref):
    @pl.when(pl.program_id(1) == 0)
    def _():
        x = x_ref[...]
        h_ref[...] = (x * _rms_scale(x, RMS_EPS) * g_ref[...]).astype(bf16)

    h = h_ref[...]
    a = _dot(h, wa_ref[...]) + ba_ref[...]
    b = _dot(h, wb_ref[...]) + bb_ref[...]
    u_ref[...] = a * _sigmoid(b)


def _pw1_glu(x, g, w, b, tm, tn):
    m, d = x.shape
    nj = d // tn
    return pl.pallas_call(
        _pw1_glu_kernel,
        grid=(m // tm, nj),
        in_specs=[
            pl.BlockSpec((tm, d), lambda i, j: (i, 0)),
            pl.BlockSpec((1, d), lambda i, j: (0, 0)),
            pl.BlockSpec((d, tn), lambda i, j: (0, j)),
            pl.BlockSpec((d, tn), lambda i, j: (0, j + nj)),
            pl.BlockSpec((1, tn), lambda i, j: (0, j)),
            pl.BlockSpec((1, tn), lambda i, j: (0, j + nj)),
        ],
        out_specs=pl.BlockSpec((tm, tn), lambda i, j: (i, j)),
        out_shape=jax.ShapeDtypeStruct((m, d), f32),
        scratch_shapes=[pltpu.VMEM((tm, d), bf16)],
        compiler_params=_params("parallel", "arbitrary"),
        name="pw1_glu",
    )(x, g, w, w, b, b)


def _conv_pw2_kernel(u_ref, uprev_ref, st_ref, x_ref, wdw_ref, bdw_ref, lng_ref, lnb_ref,
                     w2_ref, b2_ref, o_ref, full_ref, s_ref, *, tm, rc, bb):
    i = pl.program_id(1)
    first_tap = HIST_ROWS - (CONV_WIDTH - 1)
    d = u_ref.shape[2]
    cc = min(CONV_COL_CHUNK, d)

    taps_by_shift = [[(a, 8 * a + r - first_tap) for a in range(HIST_ROWS // 8 + 1)
                      if 0 <= 8 * a + r - first_tap < CONV_WIDTH] for r in range(8)]

    def fill_history(bi):
        hist = st_ref[bi] if uprev_ref is None else jnp.where(i == 0, st_ref[bi], uprev_ref[bi])
        full_ref[0:HIST_ROWS, :] = hist
        full_ref[HIST_ROWS:HIST_ROWS + tm, :] = u_ref[bi]

    def conv_chunk(bi, r0):
        pieces = []
        for c0 in range(0, d, cc):
            piece = jnp.broadcast_to(bdw_ref[:, c0:c0 + cc], (rc, cc))
            for r, taps in enumerate(taps_by_shift):
                rows = rc + (8 if r else 0)
                g = None
                for a, w in taps:
                    win = full_ref[pl.ds(_aligned(r0 + 8 * a, 8), rows), c0:c0 + cc]
                    term = win.reshape(rows // 8, 8, cc) * wdw_ref[w, :, c0:c0 + cc][None]
                    g = term if g is None else g + term
                piece = piece + g.reshape(rows, cc)[r:r + rc]
            pieces.append(piece)
        acc = jnp.concatenate(pieces, axis=1)
        mu = jnp.mean(acc, axis=-1, keepdims=True)
        cen = acc - mu
        var = jnp.mean(cen * cen, axis=-1, keepdims=True)
        y = cen * lax.rsqrt(var + LN_EPS) * lng_ref[...] + lnb_ref[...]
        s_ref[pl.ds(_aligned(bi * tm + r0, rc), rc), :] = (y * _sigmoid(y)).astype(s_ref.dtype)

    def one_sequence(bi, carry):
        fill_history(bi)
        lax.fori_loop(0, tm // rc, lambda c, carry: (conv_chunk(bi, c * rc), carry)[1], 0)
        return carry

    if bb > 1:
        lax.fori_loop(0, bb, one_sequence, 0)
        y = _dot(s_ref[...].astype(bf16), w2_ref[...]) + b2_ref[...]
        o_ref[...] = x_ref[...] + y.reshape(bb, tm, d)
        return

    fill_history(0)
    groups = max(tm // CONV_GROUP_ROWS, 1)
    rows = tm // groups

    def pointwise(g):
        sl = slice(g * rows, (g + 1) * rows)
        o_ref[0, sl, :] = x_ref[0, sl, :] + _dot(s_ref[sl, :].astype(bf16), w2_ref[...]) + b2_ref[...]

    for g in range(groups):
        if g:
            pointwise(g - 1)
        for c in range(g * rows // rc, (g + 1) * rows // rc):
            conv_chunk(0, c * rc)
    pointwise(groups - 1)


def _conv_pw2_no_prev_kernel(u_ref, st_ref, *rest, **kw):
    _conv_pw2_kernel(u_ref, None, st_ref, *rest, **kw)


def _conv_pw2(u, state_padded, x, wdw, bdw, lng, lnb, w2, b2, tm):
    bsz, t, d = u.shape
    rc = min(CONV_ROW_CHUNK, tm)
    single_tile = t == tm
    bb = bsz if (single_tile and bsz * tm <= CONV_MAX_ROWS) else 1
    tile = pl.BlockSpec((bb, tm, d), lambda b, i: (b, i, 0))
    st_spec = pl.BlockSpec((bb, HIST_ROWS, d), lambda b, i: (b, 0, 0))
    vec = pl.BlockSpec((1, d), lambda b, i: (0, 0))
    if single_tile:
        body, prev_specs, prev_args = _conv_pw2_no_prev_kernel, [], []
    else:
        hist_blocks_per_tile = tm // HIST_ROWS
        body = _conv_pw2_kernel
        prev_specs = [pl.BlockSpec(
            (bb, HIST_ROWS, d), lambda b, i: (b, jnp.maximum(i * hist_blocks_per_tile - 1, 0), 0))]
        prev_args = [u]
    s_dtype = bf16 if rc % 16 == 0 else f32
    return pl.pallas_call(
        functools.partial(body, tm=tm, rc=rc, bb=bb),
        grid=(bsz // bb, t // tm),
        in_specs=[tile] + prev_specs + [
            st_spec,
            tile,
            pl.BlockSpec((CONV_WIDTH, V7X_SUBLANES, d), lambda b, i: (0, 0, 0)),
            vec, vec, vec,
            pl.BlockSpec((d, d), lambda b, i: (0, 0), pipeline_mode=pl.Buffered(1)),
            vec,
        ],
        out_specs=tile,
        out_shape=jax.ShapeDtypeStruct((bsz, t, d), f32),
        scratch_shapes=[pltpu.VMEM((HIST_ROWS + tm, d), f32), pltpu.VMEM((bb * tm, d), s_dtype)],
        compiler_params=_params("parallel", "arbitrary"),
        name="conv_pw2",
    )(u, *prev_args, state_padded, x, wdw, bdw, lng, lnb, w2, b2)


def _ffn_kernel(x_ref, g_ref, wg_ref, wu_ref, wd_ref, fg_ref, o_ref, *refs, final_norm, emit_bf16):
    h_ref = refs[-1]
    j = pl.program_id(1)

    @pl.when(j == 0)
    def _():
        x = x_ref[...]
        h_ref[...] = (x * _rms_scale(x, RMS_EPS) * g_ref[...]).astype(bf16)
        o_ref[...] = x

    wg, wu, wd = wg_ref[...], wu_ref[...], wd_ref[...]
    if emit_bf16:
        wg, wu, wd = wg.astype(bf16), wu.astype(bf16), wd.astype(bf16)
        for out_ref, w in zip(refs[:3], (wg, wu, wd)):
            out_ref[...] = w
    h = h_ref[...]
    gate = _dot(h, wg)
    up = _dot(h, wu)
    act = (gate * _sigmoid(gate) * up).astype(bf16)
    o_ref[...] += _dot(act, wd)

    if final_norm:
        @pl.when(j == pl.num_programs(1) - 1)
        def _():
            y = o_ref[...]
            o_ref[...] = y * _rms_scale(y, RMS_EPS) * fg_ref[...]


def _ffn(x, g, wg, wu, wd, final_g, tm, tf, final_norm, layer=None):
    m, d = x.shape
    emit = layer is not None
    f = wg.shape[-1]
    if emit:
        assert m == tm
        w_in = [pl.BlockSpec((None, d, tf), lambda i, j: (layer, 0, j)),
                pl.BlockSpec((None, d, tf), lambda i, j: (layer, 0, j)),
                pl.BlockSpec((None, tf, d), lambda i, j: (layer, j, 0))]
    else:
        w_in = [pl.BlockSpec((d, tf), lambda i, j: (0, j)),
                pl.BlockSpec((d, tf), lambda i, j: (0, j)),
                pl.BlockSpec((tf, d), lambda i, j: (j, 0))]
    w_out = [pl.BlockSpec((d, tf), lambda i, j: (0, j)),
             pl.BlockSpec((d, tf), lambda i, j: (0, j)),
             pl.BlockSpec((tf, d), lambda i, j: (j, 0))] if emit else []
    w_out_shape = [jax.ShapeDtypeStruct((d, f), bf16), jax.ShapeDtypeStruct((d, f), bf16),
                   jax.ShapeDtypeStruct((f, d), bf16)] if emit else []
    out = pl.pallas_call(
        functools.partial(_ffn_kernel, final_norm=final_norm, emit_bf16=emit),
        grid=(m // tm, f // tf),
        in_specs=[
            pl.BlockSpec((tm, d), lambda i, j: (i, 0), pipeline_mode=pl.Buffered(1)),
            pl.BlockSpec((1, d), lambda i, j: (0, 0)),
            *w_in,
            pl.BlockSpec((1, d), lambda i, j: (0, 0)),
        ],
        out_specs=[pl.BlockSpec((tm, d), lambda i, j: (i, 0))] + w_out,
        out_shape=[jax.ShapeDtypeStruct((m, d), f32)] + w_out_shape,
        scratch_shapes=[pltpu.VMEM((tm, d), bf16)],
        compiler_params=_params("parallel", "arbitrary"),
        name="ffn_final" if final_norm else "ffn",
    )(x, g, wg, wu, wd, final_g)
    return out if emit else out[0]


def _qkv_kernel(x_ref, gkv_ref, gq_ref, wk_ref, wv_ref, wq_ref,
                k_ref, v_ref, q_ref, kb_ref, vb_ref, hkv_ref, hq_ref, *, q_scale):
    @pl.when(pl.program_id(1) == 0)
    def _():
        x = x_ref[...]
        xn = x * _rms_scale(x, RMS_EPS)
        hkv_ref[...] = (xn * gkv_ref[...]).astype(bf16)
        hq_ref[...] = (xn * gq_ref[...]).astype(bf16)

    hkv = hkv_ref[...]
    k = _dot(hkv, wk_ref[...])
    v = _dot(hkv, wv_ref[...])
    k_ref[...] = k
    v_ref[...] = v
    kb_ref[...] = k.astype(bf16)
    vb_ref[...] = v.astype(bf16)
    q_ref[...] = (_dot(hq_ref[...], wq_ref[...]) * q_scale).astype(bf16)


def _qkv(x, gkv, gq, wk, wv, wq, tm, tn, q_scale):
    m, d = x.shape
    wspec = pl.BlockSpec((d, tn), lambda i, j: (0, j))
    ospec = pl.BlockSpec((tm, tn), lambda i, j: (i, j))
    vec = pl.BlockSpec((1, d), lambda i, j: (0, 0))
    return pl.pallas_call(
        functools.partial(_qkv_kernel, q_scale=q_scale),
        grid=(m // tm, d // tn),
        in_specs=[pl.BlockSpec((tm, d), lambda i, j: (i, 0), pipeline_mode=pl.Buffered(1)),
                  vec, vec, wspec, wspec, wspec],
        out_specs=[ospec] * 5,
        out_shape=[jax.ShapeDtypeStruct((m, d), f32), jax.ShapeDtypeStruct((m, d), f32),
                   jax.ShapeDtypeStruct((m, d), bf16), jax.ShapeDtypeStruct((m, d), bf16),
                   jax.ShapeDtypeStruct((m, d), bf16)],
        scratch_shapes=[pltpu.VMEM((tm, d), bf16), pltpu.VMEM((tm, d), bf16)],
        compiler_params=_params("parallel", "arbitrary"),
        name="qkv_proj",
    )(x, gkv, gq, wk, wv, wq)


def _lambda(lq1_ref, lk1_ref, lq2_ref, lk2_ref, lam_init):
    e1 = jnp.exp(jnp.sum(lq1_ref[...] * lk1_ref[...], axis=-1, keepdims=True))
    e2 = jnp.exp(jnp.sum(lq2_ref[...] * lk2_ref[...], axis=-1, keepdims=True))
    return e1 - e2 + lam_init


def _lam_init():
    return 0.8 - 0.6 * math.exp(-0.3 * B_LAYER_INDEX)


def _prompt_attn_kernel(q_ref, k_ref, v_ref, lq1_ref, lk1_ref, lq2_ref, lk2_ref, sg_ref, o_ref,
                        vt_ref, qt_ref, m_ref, l_ref, acc_ref, *, tq, hd):
    i = pl.program_id(2)
    t = k_ref.shape[1]
    lam_init = _lam_init()
    lam = _lambda(lq1_ref, lk1_ref, lq2_ref, lk2_ref, lam_init)

    @pl.when(i == 0)
    def _():
        for r0 in range(0, t, tq):
            vt_ref[:, r0:r0 + tq] = v_ref[0, r0:r0 + tq, :].astype(f32).T.astype(bf16)

    for c in range(2):
        qt_ref[c] = q_ref[0, :, c * hd:(c + 1) * hd].astype(f32).T.astype(bf16)
    m_ref[...] = jnp.full(m_ref.shape, NEG_INF, f32)
    l_ref[...] = jnp.zeros(l_ref.shape, f32)
    acc_ref[...] = jnp.zeros(acc_ref.shape, f32)

    def block(j, masked):
        r0 = pl.multiple_of(j * tq, tq)
        tk = tq // ATTN_KEY_SPLIT
        kblk = k_ref[0, pl.ds(r0, tq), :]
        vtblk = vt_ref[:, pl.ds(r0, tq)]
        s = [[_dot(kblk[a * tk:(a + 1) * tk, c * hd:(c + 1) * hd], qt_ref[c]) for c in range(2)]
             for a in range(ATTN_KEY_SPLIT)]
        for a in range(ATTN_KEY_SPLIT):
            if masked:
                key = lax.broadcasted_iota(jnp.int32, (tk, tq), 0) + a * tk
                query = lax.broadcasted_iota(jnp.int32, (tk, tq), 1)
                keep = key <= query
            for c in range(2):
                sc = jnp.where(keep, s[a][c], NEG_INF) if masked else s[a][c]
                m_old = m_ref[c]
                m_new = jnp.maximum(m_old, jnp.max(sc, axis=0, keepdims=True))
                alpha = jnp.exp2(m_old - m_new)
                p = jnp.exp2(sc - m_new[0:1])
                l_ref[c] = alpha * l_ref[c] + jnp.sum(p, axis=0, keepdims=True)
                acc_ref[c] = alpha[0:1] * acc_ref[c] + _dot(vtblk[:, a * tk:(a + 1) * tk], p.astype(bf16))
                m_ref[c] = m_new

    def body(j, carry):
        block(j, masked=False)
        return carry

    lax.fori_loop(0, i, body, 0)
    block(i, masked=True)

    ot = acc_ref[0] * (1.0 / l_ref[0][0:1]) - lam * (acc_ref[1] * (1.0 / l_ref[1][0:1]))
    o = ot.T
    o = o * _rms_scale(o, SUBLN_EPS) * sg_ref[...] * (1.0 - lam_init)
    o_ref[0] = o.astype(bf16)


def _prompt_attn(q, k, v, lq1, lk1, lq2, lk2, sg, tq):
    bsz, t, d = q.shape
    hd = d // N_HEADS // 2
    vd = 2 * hd
    lvec = pl.BlockSpec((1, hd), lambda b, h, i: (0, 0))
    return pl.pallas_call(
        functools.partial(_prompt_attn_kernel, tq=tq, hd=hd),
        grid=(bsz, N_HEADS, t // tq),
        in_specs=[
            pl.BlockSpec((1, tq, vd), lambda b, h, i: (b, i, h)),
            pl.BlockSpec((1, t, vd), lambda b, h, i: (b, 0, h)),
            pl.BlockSpec((1, t, vd), lambda b, h, i: (b, 0, h)),
            lvec, lvec, lvec, lvec,
            pl.BlockSpec((1, vd), lambda b, h, i: (0, 0)),
        ],
        out_specs=pl.BlockSpec((1, tq, vd), lambda b, h, i: (b, i, h)),
        out_shape=jax.ShapeDtypeStruct((bsz, t, d), bf16),
        scratch_shapes=[pltpu.VMEM((vd, t), bf16), pltpu.VMEM((2, hd, tq), bf16),
                        pltpu.VMEM((2, V7X_SUBLANES, tq), f32), pltpu.VMEM((2, V7X_SUBLANES, tq), f32),
                        pltpu.VMEM((2, vd, tq), f32)],
        compiler_params=_params("parallel", "parallel", "arbitrary"),
        name="prompt_diff_attn",
    )(q, k, v, lq1, lk1, lq2, lk2, sg)


def _sample_attn_kernel(pt_ref, q_ref, kn_ref, vn_ref, lq1_ref, lk1_ref, lq2_ref, lk2_ref, sg_ref,
                        *refs, pages, tq, hd):
    k_refs = refs[:pages]
    v_refs = refs[pages:2 * pages]
    o_ref = refs[2 * pages]
    qbd_ref, bias_ref, m_ref, l_ref, acc_ref = refs[2 * pages + 1:]
    t = pl.program_id(1)
    vd = 2 * hd
    nl = 2 * N_HEADS * tq
    half = N_HEADS * tq
    page = k_refs[0].shape[1]
    lam_init = _lam_init()

    @pl.when(t == 0)
    def _():
        qf = q_ref[0].astype(f32)
        zero = jnp.zeros((hd, half), f32)
        halves = []
        for c in range(2):
            qc = jnp.concatenate([qf[:, (2 * h + c) * hd:(2 * h + c + 1) * hd] for h in range(N_HEADS)], axis=0).T
            halves.append(jnp.concatenate([qc, zero] if c == 0 else [zero, qc], axis=1))
        qbd_ref[...] = jnp.concatenate(halves, axis=0)
        head = lax.broadcasted_iota(jnp.int32, (N_HEADS, nl), 0)
        lane = lax.broadcasted_iota(jnp.int32, (N_HEADS, nl), 1)
        bias_ref[...] = jnp.where((lane % half) // tq == head, 0.0, NEG_INF)
        m_ref[...] = jnp.full(m_ref.shape, NEG_INF, f32)
        l_ref[...] = jnp.zeros(l_ref.shape, f32)
        acc_ref[...] = jnp.zeros(acc_ref.shape, f32)

    def scores(k0, k1):
        s = _dot(jnp.concatenate([k0, k1], axis=1), qbd_ref[...])
        return s.reshape(s.shape[0] // N_HEADS, N_HEADS, nl) + bias_ref[...][None]

    def update(s, v2):
        m_old = m_ref[...]
        m_new = jnp.maximum(m_old, jnp.max(jnp.max(s, axis=0), axis=0, keepdims=True))
        alpha = jnp.exp2(m_old - m_new)
        p = jnp.exp2(s - m_new[None])
        l_ref[...] = alpha * l_ref[...] + jnp.sum(jnp.sum(p, axis=0), axis=0, keepdims=True)
        m_ref[...] = m_new
        pv = lax.dot_general(p.reshape(v2.shape[0], nl).astype(bf16), v2.astype(bf16),
                             (((0,), (0,)), ((), ())), preferred_element_type=f32)
        alpha_col = jnp.broadcast_to(alpha[0:1], (nl, nl)).T
        acc_ref[...] = jnp.concatenate([alpha_col] * (vd // nl), axis=1) * acc_ref[...] + pv

    n = page * N_HEADS
    page_scores = [scores(*(r[0, :, pl.ds(c, N_HEADS, stride=2), :].reshape(n, hd) for c in range(2)))
                   for r in k_refs]
    for s, r in zip(page_scores, v_refs):
        update(s, r[0].reshape(n, vd))

    @pl.when(t == pl.num_programs(1) - 1)
    def _():
        tok = lax.broadcasted_iota(jnp.int32, (tq, N_HEADS, nl), 0)
        qi = lax.broadcasted_iota(jnp.int32, (tq, N_HEADS, nl), 2) % tq
        sn = scores(*(kn_ref[0, pl.ds(c, tq * N_HEADS, stride=2), :] for c in range(2)))
        update(sn + jnp.where(tok <= qi, 0.0, NEG_INF), vn_ref[0])

        lam = _lambda(lq1_ref, lk1_ref, lq2_ref, lk2_ref, lam_init)
        inv_l = jnp.broadcast_to(1.0 / l_ref[0:1], (nl, nl)).T
        o = acc_ref[0:half] * inv_l[0:half, 0:1] - lam * (acc_ref[half:nl] * inv_l[half:nl, 0:1])
        o = o * _rms_scale(o, SUBLN_EPS) * sg_ref[...] * (1.0 - lam_init)
        for h in range(N_HEADS):
            o_ref[0, :, h * vd:(h + 1) * vd] = o[h * tq:(h + 1) * tq].astype(bf16)


def _sample_attn(page_table, q, k_new, v_new, cache_k, cache_v, lq1, lk1, lq2, lk2, sg, pages):
    nseq, tq, d = q.shape
    hd = d // N_HEADS // 2
    n_pages = page_table.shape[1]
    page = cache_k.shape[1]
    nl = 2 * N_HEADS * tq
    assert nl == V7X_LANES and hd == V7X_LANES and n_pages % pages == 0

    def k_spec(p):
        return pl.BlockSpec((1, page, 2 * N_HEADS, hd), lambda s, t, pt: (pt[s, t * pages + p], 0, 0, 0))

    def v_spec(p):
        return pl.BlockSpec((1, page, N_HEADS, 2 * hd), lambda s, t, pt: (pt[s, t * pages + p], 0, 0, 0))

    lvec = pl.BlockSpec((1, hd), lambda s, t, pt: (0, 0))
    grid_spec = pltpu.PrefetchScalarGridSpec(
        num_scalar_prefetch=1,
        grid=(nseq, n_pages // pages),
        in_specs=[pl.BlockSpec((1, tq, d), lambda s, t, pt: (s, 0, 0)),
                  pl.BlockSpec((1, tq * 2 * N_HEADS, hd), lambda s, t, pt: (s, 0, 0)),
                  pl.BlockSpec((1, tq * N_HEADS, 2 * hd), lambda s, t, pt: (s, 0, 0)),
                  lvec, lvec, lvec, lvec,
                  pl.BlockSpec((1, 2 * hd), lambda s, t, pt: (0, 0))]
                 + [k_spec(p) for p in range(pages)] + [v_spec(p) for p in range(pages)],
        out_specs=pl.BlockSpec((1, tq, d), lambda s, t, pt: (s, 0, 0)),
        scratch_shapes=[pltpu.VMEM((2 * hd, nl), f32), pltpu.VMEM((N_HEADS, nl), f32),
                        pltpu.VMEM((V7X_SUBLANES, nl), f32), pltpu.VMEM((V7X_SUBLANES, nl), f32),
                        pltpu.VMEM((nl, 2 * hd), f32)],
    )
    return pl.pallas_call(
        functools.partial(_sample_attn_kernel, pages=pages, tq=tq, hd=hd),
        grid_spec=grid_spec,
        out_shape=jax.ShapeDtypeStruct((nseq, tq, d), bf16),
        compiler_params=_params("parallel", "arbitrary"),
        name="sample_paged_diff_attn",
    )(page_table, q, k_new, v_new, lq1, lk1, lq2, lk2, sg,
      *([cache_k] * pages), *([cache_v] * pages))


def _oproj_kernel(o_ref, w_ref, x_ref, y_ref):
    y_ref[...] = x_ref[...] + _dot(o_ref[...], w_ref[...])


def _oproj(o, w, x, tm, tn):
    m, d = x.shape
    return pl.pallas_call(
        _oproj_kernel,
        grid=(m // tm, d // tn),
        in_specs=[pl.BlockSpec((tm, o.shape[1]), lambda i, j: (i, 0)),
                  pl.BlockSpec((o.shape[1], tn), lambda i, j: (0, j)),
                  pl.BlockSpec((tm, tn), lambda i, j: (i, j))],
        out_specs=pl.BlockSpec((tm, tn), lambda i, j: (i, j)),
        out_shape=jax.ShapeDtypeStruct((m, d), f32),
        compiler_params=_params("parallel", "arbitrary"),
        name="attn_out_proj",
    )(o, w, x)


def _trunk(x, state, attend, p, tm, conv_tm, ffn_bf16=None):
    bsz, t, d = x.shape
    m = bsz * t
    hd = d // N_HEADS // 2
    tn = min(512, d)
    x2 = x.reshape(m, d)

    u = _pw1_glu(x2, p['attn_norm_g'][0:1], p['conv_w_pw1'], p['conv_b_pw1'], tm, tn)
    u3 = u.reshape(bsz, t, d)
    state_padded = jnp.pad(state, ((0, 0), (HIST_ROWS - (CONV_WIDTH - 1), 0), (0, 0)))
    x1 = _conv_pw2(u3, state_padded, x, p['conv_w_dw'], p['conv_b_dw'], p['conv_ln_g'],
                   p['conv_ln_b'], p['conv_w_pw2'], p['conv_b_pw2'], conv_tm)
    new_state = jnp.concatenate([state, u3], axis=1)[:, t:]

    def ffn(xin, layer, final_norm):
        args = (p['ffn_norm_g'][layer:layer + 1],)
        if ffn_bf16 is None:
            y, *w = _ffn(xin, *args, p['ffn_w_gate'], p['ffn_w_up'], p['ffn_w_down'], p['final_norm_g'],
                         tm, FFN_TILE, final_norm, layer=layer)
            return y, tuple(w)
        return _ffn(xin, *args, *ffn_bf16[layer], p['final_norm_g'], tm, FFN_TILE, final_norm), ffn_bf16[layer]

    x1, w_layer0 = ffn(x1.reshape(m, d), 0, False)

    k, v, q, kb, vb = _qkv(x1, p['kv_norm_g'], p['attn_norm_g'][1:2], p['w_k'], p['w_v'],
                           p['attn_w_q'], tm, tn // 2, hd ** -0.5 * math.log2(math.e))
    o = attend(q.reshape(bsz, t, d), k.reshape(bsz, t, d), v.reshape(bsz, t, d),
               kb.reshape(bsz, t, d), vb.reshape(bsz, t, d))
    x2 = _oproj(o.reshape(m, d), p['attn_w_o'], x1, tm, tn)
    y, w_layer1 = ffn(x2, 1, True)
    return (y.reshape(bsz, t, d), new_state[None],
            k.reshape(bsz, t, 2 * N_HEADS, hd), v.reshape(bsz, t, N_HEADS, 2 * hd), (w_layer0, w_layer1))


def kernel(x_prompt, x_sample, state_conv, cache_k, cache_v, page_table, attn_norm_g, ffn_norm_g,
           conv_w_pw1, conv_b_pw1, conv_w_dw, conv_b_dw, conv_ln_g, conv_ln_b, conv_w_pw2,
           conv_b_pw2, kv_norm_g, w_k, w_v, attn_w_q, attn_lambda_q1, attn_lambda_k1,
           attn_lambda_q2, attn_lambda_k2, attn_subln_g, attn_w_o, ffn_w_gate, ffn_w_up,
           ffn_w_down, final_norm_g):
    assert attn_norm_g.shape[0] == 2 and conv_w_pw1.shape[0] == 1 and attn_w_q.shape[0] == 1
    d = x_prompt.shape[-1]
    p = dict(
        attn_norm_g=attn_norm_g, ffn_norm_g=ffn_norm_g,
        conv_w_pw1=conv_w_pw1[0].astype(bf16), conv_b_pw1=conv_b_pw1,
        conv_w_dw=jnp.broadcast_to(conv_w_dw[0][:, None, :], (CONV_WIDTH, V7X_SUBLANES, d)),
        conv_b_dw=conv_b_dw, conv_ln_g=conv_ln_g, conv_ln_b=conv_ln_b,
        conv_w_pw2=conv_w_pw2[0].astype(bf16), conv_b_pw2=conv_b_pw2,
        kv_norm_g=kv_norm_g.reshape(1, d), w_k=w_k.astype(bf16), w_v=w_v.astype(bf16),
        attn_w_q=attn_w_q[0].astype(bf16), attn_w_o=attn_w_o[0].astype(bf16),
        ffn_w_gate=ffn_w_gate, ffn_w_up=ffn_w_up, ffn_w_down=ffn_w_down,
        final_norm_g=final_norm_g.reshape(1, d),
    )
    lams = (attn_lambda_q1, attn_lambda_k1, attn_lambda_q2, attn_lambda_k2, attn_subln_g)

    def attend_prompt(q, k, v, kb, vb):
        return _prompt_attn(q, kb, vb, *lams, tq=512)

    def attend_sample(q, k, v, kb, vb):
        s, tq, _ = q.shape
        hd = d // N_HEADS // 2
        k_rows = k.reshape(s, tq * 2 * N_HEADS, hd)
        v_rows = v.reshape(s, tq * N_HEADS, 2 * hd)
        return _sample_attn(page_table, q, k_rows, v_rows, cache_k, cache_v, *lams, pages=4)

    sb, st, _ = x_sample.shape
    y_s, conv_s, k_s, v_s, ffn_bf16 = _trunk(x_sample, state_conv[0], attend_sample, p, tm=sb * st, conv_tm=st)
    bsz, t, _ = x_prompt.shape
    conv0 = jnp.zeros((bsz, CONV_WIDTH - 1, d), x_prompt.dtype)
    y_p, conv_p, k_p, v_p, _ = _trunk(x_prompt, conv0, attend_prompt, p, tm=1024, conv_tm=512,
                                      ffn_bf16=ffn_bf16)
    return (y_p, y_s, conv_p, conv_s, k_p, v_p, k_s, v_s)
```

```python
import functools
import math

import jax
import jax.numpy as jnp
from jax import lax
from jax.experimental import pallas as pl
from jax.experimental.pallas import tpu as pltpu

N_HEADS = 8
CONV_WIDTH = 31
RMS_EPS = 1e-6
SUBLN_EPS = 1e-5
LN_EPS = 1e-5
NEG_INF = -1e30
B_LAYER_INDEX = 1

V7X_VMEM_BYTES = 64 * 1024 * 1024
V7X_SUBLANES = 8
V7X_LANES = 128
VMEM_LIMIT_BYTES = V7X_VMEM_BYTES - 8 * 1024 * 1024

HIST_ROWS = 32
CONV_ROW_CHUNK = 32
CONV_COL_CHUNK = 256
CONV_GROUP_ROWS = 128
CONV_MAX_ROWS = 1024
ATTN_KEY_SPLIT = 2
FFN_TILE = 512
PROMPT_ROW_TILE = 1024
FUSED_FFN_TILE = 256

f32 = jnp.float32
bf16 = jnp.bfloat16


def _params(*sem):
    return pltpu.CompilerParams(dimension_semantics=sem, vmem_limit_bytes=VMEM_LIMIT_BYTES)


def _rms_scale(x, eps):
    return lax.rsqrt(jnp.mean(x * x, axis=-1, keepdims=True) + eps)


def _sigmoid(x):
    return 1.0 / (1.0 + jnp.exp(-x))


def _aligned(i, n):
    return i if isinstance(i, int) else pl.multiple_of(i, n)


def _dot(a, b):
    return jnp.dot(a, b, preferred_element_type=f32)


def _pw1_glu_kernel(x_ref, g_ref, wa_ref, wb_ref, ba_ref, bb_ref, u_ref, h_ref):
    @pl.when(pl.program_id(1) == 0)
    def _():
        x = x_ref[...]
        h_ref[...] = (x * _rms_scale(x, RMS_EPS) * g_ref[...]).astype(bf16)

    h = h_ref[...]
    a = _dot(h, wa_ref[...]) + ba_ref[...]
    b = _dot(h, wb_ref[...]) + bb_ref[...]
    u_ref[...] = a * _sigmoid(b)


def _pw1_glu(x, g, w, b, tm, tn):
    m, d = x.shape
    nj = d // tn
    return pl.pallas_call(
        _pw1_glu_kernel,
        grid=(m // tm, nj),
        in_specs=[
            pl.BlockSpec((tm, d), lambda i, j: (i, 0)),
            pl.BlockSpec((1, d), lambda i, j: (0, 0)),
            pl.BlockSpec((d, tn), lambda i, j: (0, j)),
            pl.BlockSpec((d, tn), lambda i, j: (0, j + nj)),
            pl.BlockSpec((1, tn), lambda i, j: (0, j)),
            pl.BlockSpec((1, tn), lambda i, j: (0, j + nj)),
        ],
        out_specs=pl.BlockSpec((tm, tn), lambda i, j: (i, j)),
        out_shape=jax.ShapeDtypeStruct((m, d), f32),
        scratch_shapes=[pltpu.VMEM((tm, d), bf16)],
        compiler_params=_params("parallel", "arbitrary"),
        name="pw1_glu",
    )(x, g, w, w, b, b)


def _conv_pw2_kernel(u_ref, uprev_ref, st_ref, x_ref, wdw_ref, bdw_ref, lng_ref, lnb_ref,
                     w2_ref, b2_ref, o_ref, full_ref, s_ref, *, tm, rc, bb):
    i = pl.program_id(1)
    first_tap = HIST_ROWS - (CONV_WIDTH - 1)
    d = u_ref.shape[2]
    cc = min(CONV_COL_CHUNK, d)

    taps_by_shift = [[(a, 8 * a + r - first_tap) for a in range(HIST_ROWS // 8 + 1)
                      if 0 <= 8 * a + r - first_tap < CONV_WIDTH] for r in range(8)]

    def fill_history(bi):
        hist = st_ref[bi] if uprev_ref is None else jnp.where(i == 0, st_ref[bi], uprev_ref[bi])
        full_ref[0:HIST_ROWS, :] = hist
        full_ref[HIST_ROWS:HIST_ROWS + tm, :] = u_ref[bi]

    def conv_chunk(bi, r0):
        pieces = []
        for c0 in range(0, d, cc):
            piece = jnp.broadcast_to(bdw_ref[:, c0:c0 + cc], (rc, cc))
            for r, taps in enumerate(taps_by_shift):
                rows = rc + (8 if r else 0)
                g = None
                for a, w in taps:
                    win = full_ref[pl.ds(_aligned(r0 + 8 * a, 8), rows), c0:c0 + cc]
                    term = win.reshape(rows // 8, 8, cc) * wdw_ref[w, :, c0:c0 + cc][None]
                    g = term if g is None else g + term
                piece = piece + g.reshape(rows, cc)[r:r + rc]
            pieces.append(piece)
        acc = jnp.concatenate(pieces, axis=1)
        mu = jnp.mean(acc, axis=-1, keepdims=True)
        cen = acc - mu
        var = jnp.mean(cen * cen, axis=-1, keepdims=True)
        y = cen * lax.rsqrt(var + LN_EPS) * lng_ref[...] + lnb_ref[...]
        s_ref[pl.ds(_aligned(bi * tm + r0, rc), rc), :] = (y * _sigmoid(y)).astype(s_ref.dtype)

    def one_sequence(bi, carry):
        fill_history(bi)
        lax.fori_loop(0, tm // rc, lambda c, carry: (conv_chunk(bi, c * rc), carry)[1], 0)
        return carry

    if bb > 1:
        lax.fori_loop(0, bb, one_sequence, 0)
        y = _dot(s_ref[...].astype(bf16), w2_ref[...]) + b2_ref[...]
        o_ref[...] = x_ref[...] + y.reshape(bb, tm, d)
        return

    fill_history(0)
    groups = max(tm // CONV_GROUP_ROWS, 1)
    rows = tm // groups

    def pointwise(g):
        sl = slice(g * rows, (g + 1) * rows)
        o_ref[0, sl, :] = x_ref[0, sl, :] + _dot(s_ref[sl, :].astype(bf16), w2_ref[...]) + b2_ref[...]

    for g in range(groups):
        if g:
            pointwise(g - 1)
        for c in range(g * rows // rc, (g + 1) * rows // rc):
            conv_chunk(0, c * rc)
    pointwise(groups - 1)


def _conv_pw2_no_prev_kernel(u_ref, st_ref, *rest, **kw):
    _conv_pw2_kernel(u_ref, None, st_ref, *rest, **kw)


def _conv_pw2(u, state_padded, x, wdw, bdw, lng, lnb, w2, b2, tm):
    bsz, t, d = u.shape
    rc = min(CONV_ROW_CHUNK, tm)
    single_tile = t == tm
    bb = bsz if (single_tile and bsz * tm <= CONV_MAX_ROWS) else 1
    tile = pl.BlockSpec((bb, tm, d), lambda b, i: (b, i, 0))
    st_spec = pl.BlockSpec((bb, HIST_ROWS, d), lambda b, i: (b, 0, 0))
    vec = pl.BlockSpec((1, d), lambda b, i: (0, 0))
    if single_tile:
        body, prev_specs, prev_args = _conv_pw2_no_prev_kernel, [], []
    else:
        hist_blocks_per_tile = tm // HIST_ROWS
        body = _conv_pw2_kernel
        prev_specs = [pl.BlockSpec(
            (bb, HIST_ROWS, d), lambda b, i: (b, jnp.maximum(i * hist_blocks_per_tile - 1, 0), 0))]
        prev_args = [u]
    s_dtype = bf16 if rc % 16 == 0 else f32
    return pl.pallas_call(
        functools.partial(body, tm=tm, rc=rc, bb=bb),
        grid=(bsz // bb, t // tm),
        in_specs=[tile] + prev_specs + [
            st_spec,
            tile,
            pl.BlockSpec((CONV_WIDTH, V7X_SUBLANES, d), lambda b, i: (0, 0, 0)),
            vec, vec, vec,
            pl.BlockSpec((d, d), lambda b, i: (0, 0), pipeline_mode=pl.Buffered(1)),
            vec,
        ],
        out_specs=tile,
        out_shape=jax.ShapeDtypeStruct((bsz, t, d), f32),
        scratch_shapes=[pltpu.VMEM((HIST_ROWS + tm, d), f32), pltpu.VMEM((bb * tm, d), s_dtype)],
        compiler_params=_params("parallel", "arbitrary"),
        name="conv_pw2",
    )(u, *prev_args, state_padded, x, wdw, bdw, lng, lnb, w2, b2)


def _ffn_kernel(x_ref, g_ref, wg_ref, wu_ref, wd_ref, fg_ref, o_ref, *refs, final_norm, emit_bf16):
    h_ref = refs[-1]
    j = pl.program_id(1)

    @pl.when(j == 0)
    def _():
        x = x_ref[...]
        h_ref[...] = (x * _rms_scale(x, RMS_EPS) * g_ref[...]).astype(bf16)
        o_ref[...] = x

    wg, wu, wd = wg_ref[...], wu_ref[...], wd_ref[...]
    if emit_bf16:
        wg, wu, wd = wg.astype(bf16), wu.astype(bf16), wd.astype(bf16)
        for out_ref, w in zip(refs[:3], (wg, wu, wd)):
            out_ref[...] = w
    h = h_ref[...]
    gate = _dot(h, wg)
    up = _dot(h, wu)
    act = (gate * _sigmoid(gate) * up).astype(bf16)
    o_ref[...] += _dot(act, wd)

    if final_norm:
        @pl.when(j == pl.num_programs(1) - 1)
        def _():
            y = o_ref[...]
            o_ref[...] = y * _rms_scale(y, RMS_EPS) * fg_ref[...]


def _ffn(x, g, wg, wu, wd, final_g, tm, tf, final_norm, layer=None):
    m, d = x.shape
    emit = layer is not None
    f = wg.shape[-1]
    if emit:
        assert m == tm
        w_in = [pl.BlockSpec((None, d, tf), lambda i, j: (layer, 0, j)),
                pl.BlockSpec((None, d, tf), lambda i, j: (layer, 0, j)),
                pl.BlockSpec((None, tf, d), lambda i, j: (layer, j, 0))]
    else:
        w_in = [pl.BlockSpec((d, tf), lambda i, j: (0, j)),
                pl.BlockSpec((d, tf), lambda i, j: (0, j)),
                pl.BlockSpec((tf, d), lambda i, j: (j, 0))]
    w_out = [pl.BlockSpec((d, tf), lambda i, j: (0, j)),
             pl.BlockSpec((d, tf), lambda i, j: (0, j)),
             pl.BlockSpec((tf, d), lambda i, j: (j, 0))] if emit else []
    w_out_shape = [jax.ShapeDtypeStruct((d, f), bf16), jax.ShapeDtypeStruct((d, f), bf16),
                   jax.ShapeDtypeStruct((f, d), bf16)] if emit else []
    out = pl.pallas_call(
        functools.partial(_ffn_kernel, final_norm=final_norm, emit_bf16=emit),
        grid=(m // tm, f // tf),
        in_specs=[
            pl.BlockSpec((tm, d), lambda i, j: (i, 0), pipeline_mode=pl.Buffered(1)),
            pl.BlockSpec((1, d), lambda i, j: (0, 0)),
            *w_in,
            pl.BlockSpec((1, d), lambda i, j: (0, 0)),
        ],
        out_specs=[pl.BlockSpec((tm, d), lambda i, j: (i, 0))] + w_out,
        out_shape=[jax.ShapeDtypeStruct((m, d), f32)] + w_out_shape,
        scratch_shapes=[pltpu.VMEM((tm, d), bf16)],
        compiler_params=_params("parallel", "arbitrary"),
        name="ffn_final" if final_norm else "ffn",
    )(x, g, wg, wu, wd, final_g)
    return out if emit else out[0]


def _qkv_kernel(x_ref, gkv_ref, gq_ref, wk_ref, wv_ref, wq_ref,
                k_ref, v_ref, q_ref, kb_ref, vb_ref, hkv_ref, hq_ref, *, q_scale):
    @pl.when(pl.program_id(1) == 0)
    def _():
        x = x_ref[...]
        xn = x * _rms_scale(x, RMS_EPS)
        hkv_ref[...] = (xn * gkv_ref[...]).astype(bf16)
        hq_ref[...] = (xn * gq_ref[...]).astype(bf16)

    hkv = hkv_ref[...]
    k = _dot(hkv, wk_ref[...])
    v = _dot(hkv, wv_ref[...])
    k_ref[...] = k
    v_ref[...] = v
    kb_ref[...] = k.astype(bf16)
    vb_ref[...] = v.astype(bf16)
    q_ref[...] = (_dot(hq_ref[...], wq_ref[...]) * q_scale).astype(bf16)


def _qkv(x, gkv, gq, wk, wv, wq, tm, tn, q_scale):
    m, d = x.shape
    wspec = pl.BlockSpec((d, tn), lambda i, j: (0, j))
    ospec = pl.BlockSpec((tm, tn), lambda i, j: (i, j))
    vec = pl.BlockSpec((1, d), lambda i, j: (0, 0))
    return pl.pallas_call(
        functools.partial(_qkv_kernel, q_scale=q_scale),
        grid=(m // tm, d // tn),
        in_specs=[pl.BlockSpec((tm, d), lambda i, j: (i, 0), pipeline_mode=pl.Buffered(1)),
                  vec, vec, wspec, wspec, wspec],
        out_specs=[ospec] * 5,
        out_shape=[jax.ShapeDtypeStruct((m, d), f32), jax.ShapeDtypeStruct((m, d), f32),
                   jax.ShapeDtypeStruct((m, d), bf16), jax.ShapeDtypeStruct((m, d), bf16),
                   jax.ShapeDtypeStruct((m, d), bf16)],
        scratch_shapes=[pltpu.VMEM((tm, d), bf16), pltpu.VMEM((tm, d), bf16)],
        compiler_params=_params("parallel", "arbitrary"),
        name="qkv_proj",
    )(x, gkv, gq, wk, wv, wq)


def _lambda(lq1_ref, lk1_ref, lq2_ref, lk2_ref, lam_init):
    e1 = jnp.exp(jnp.sum(lq1_ref[...] * lk1_ref[...], axis=-1, keepdims=True))
    e2 = jnp.exp(jnp.sum(lq2_ref[...] * lk2_ref[...], axis=-1, keepdims=True))
    return e1 - e2 + lam_init


def _lam_init():
    return 0.8 - 0.6 * math.exp(-0.3 * B_LAYER_INDEX)


def _prompt_attn_kernel(q_ref, k_ref, v_ref, lq1_ref, lk1_ref, lq2_ref, lk2_ref, sg_ref, o_ref,
                        vt_ref, qt_ref, m_ref, l_ref, acc_ref, *, tq, hd):
    i = pl.program_id(2)
    t = k_ref.shape[1]
    lam_init = _lam_init()
    lam = _lambda(lq1_ref, lk1_ref, lq2_ref, lk2_ref, lam_init)

    @pl.when(i == 0)
    def _():
        for r0 in range(0, t, tq):
            vt_ref[:, r0:r0 + tq] = v_ref[0, r0:r0 + tq, :].astype(f32).T.astype(bf16)

    for c in range(2):
        qt_ref[c] = q_ref[0, :, c * hd:(c + 1) * hd].astype(f32).T.astype(bf16)
    m_ref[...] = jnp.full(m_ref.shape, NEG_INF, f32)
    l_ref[...] = jnp.zeros(l_ref.shape, f32)
    acc_ref[...] = jnp.zeros(acc_ref.shape, f32)

    def block(j, masked):
        r0 = pl.multiple_of(j * tq, tq)
        tk = tq // ATTN_KEY_SPLIT
        kblk = k_ref[0, pl.ds(r0, tq), :]
        vtblk = vt_ref[:, pl.ds(r0, tq)]
        s = [[_dot(kblk[a * tk:(a + 1) * tk, c * hd:(c + 1) * hd], qt_ref[c]) for c in range(2)]
             for a in range(ATTN_KEY_SPLIT)]
        for a in range(ATTN_KEY_SPLIT):
            if masked:
                key = lax.broadcasted_iota(jnp.int32, (tk, tq), 0) + a * tk
                query = lax.broadcasted_iota(jnp.int32, (tk, tq), 1)
                keep = key <= query
            for c in range(2):
                sc = jnp.where(keep, s[a][c], NEG_INF) if masked else s[a][c]
                m_old = m_ref[c]
                m_new = jnp.maximum(m_old, jnp.max(sc, axis=0, keepdims=True))
                alpha = jnp.exp2(m_old - m_new)
                p = jnp.exp2(sc - m_new[0:1])
                l_ref[c] = alpha * l_ref[c] + jnp.sum(p, axis=0, keepdims=True)
                acc_ref[c] = alpha[0:1] * acc_ref[c] + _dot(vtblk[:, a * tk:(a + 1) * tk], p.astype(bf16))
                m_ref[c] = m_new

    def body(j, carry):
        block(j, masked=False)
        return carry

    lax.fori_loop(0, i, body, 0)
    block(i, masked=True)

    ot = acc_ref[0] * (1.0 / l_ref[0][0:1]) - lam * (acc_ref[1] * (1.0 / l_ref[1][0:1]))
    o = ot.T
    o = o * _rms_scale(o, SUBLN_EPS) * sg_ref[...] * (1.0 - lam_init)
    o_ref[0] = o.astype(bf16)


def _prompt_attn(q, k, v, lq1, lk1, lq2, lk2, sg, tq):
    bsz, t, d = q.shape
    hd = d // N_HEADS // 2
    vd = 2 * hd
    lvec = pl.BlockSpec((1, hd), lambda b, h, i: (0, 0))
    return pl.pallas_call(
        functools.partial(_prompt_attn_kernel, tq=tq, hd=hd),
        grid=(bsz, N_HEADS, t // tq),
        in_specs=[
            pl.BlockSpec((1, tq, vd), lambda b, h, i: (b, i, h)),
            pl.BlockSpec((1, t, vd), lambda b, h, i: (b, 0, h)),
            pl.BlockSpec((1, t, vd), lambda b, h, i: (b, 0, h)),
            lvec, lvec, lvec, lvec,
            pl.BlockSpec((1, vd), lambda b, h, i: (0, 0)),
        ],
        out_specs=pl.BlockSpec((1, tq, vd), lambda b, h, i: (b, i, h)),
        out_shape=jax.ShapeDtypeStruct((bsz, t, d), bf16),
        scratch_shapes=[pltpu.VMEM((vd, t), bf16), pltpu.VMEM((2, hd, tq), bf16),
                        pltpu.VMEM((2, V7X_SUBLANES, tq), f32), pltpu.VMEM((2, V7X_SUBLANES, tq), f32),
                        pltpu.VMEM((2, vd, tq), f32)],
        compiler_params=_params("parallel", "parallel", "arbitrary"),
        name="prompt_diff_attn",
    )(q, k, v, lq1, lk1, lq2, lk2, sg)


def _paged_attn(q_ref, kn_ref, vn_ref, lam_refs, sg_ref, k_refs, v_refs, o_ref, scratch, *, tq, hd):
    qbd_ref, bias_ref, m_ref, l_ref, acc_ref = scratch
    vd = 2 * hd
    nl = 2 * N_HEADS * tq
    half = N_HEADS * tq
    page = k_refs[0].shape[1]
    lam_init = _lam_init()

    def init():
        qf = q_ref[0].astype(f32)
        zero = jnp.zeros((hd, half), f32)
        halves = []
        for c in range(2):
            qc = jnp.concatenate([qf[:, (2 * h + c) * hd:(2 * h + c + 1) * hd] for h in range(N_HEADS)], axis=0).T
            halves.append(jnp.concatenate([qc, zero] if c == 0 else [zero, qc], axis=1))
        qbd_ref[...] = jnp.concatenate(halves, axis=0)
        head = lax.broadcasted_iota(jnp.int32, (N_HEADS, nl), 0)
        lane = lax.broadcasted_iota(jnp.int32, (N_HEADS, nl), 1)
        bias_ref[...] = jnp.where((lane % half) // tq == head, 0.0, NEG_INF)
        m_ref[...] = jnp.full(m_ref.shape, NEG_INF, f32)
        l_ref[...] = jnp.zeros(l_ref.shape, f32)
        acc_ref[...] = jnp.zeros(acc_ref.shape, f32)

    def scores(k0, k1, slot_bias=0.0):
        s = _dot(jnp.concatenate([k0, k1], axis=1), qbd_ref[...])
        return s.reshape(s.shape[0] // N_HEADS, N_HEADS, nl) + (bias_ref[...] + slot_bias)[None]

    def update(s, v2):
        m_old = m_ref[...]
        m_new = jnp.maximum(m_old, jnp.max(jnp.max(s, axis=0), axis=0, keepdims=True))
        alpha = jnp.exp2(m_old - m_new)
        p = jnp.exp2(s - m_new[None])
        l_ref[...] = alpha * l_ref[...] + jnp.sum(jnp.sum(p, axis=0), axis=0, keepdims=True)
        m_ref[...] = m_new
        pv = lax.dot_general(p.reshape(v2.shape[0], nl).astype(bf16), v2.astype(bf16),
                             (((0,), (0,)), ((), ())), preferred_element_type=f32)
        alpha_col = jnp.broadcast_to(alpha[0:1], (nl, nl)).T
        acc_ref[...] = jnp.concatenate([alpha_col] * (vd // nl), axis=1) * acc_ref[...] + pv

    n = page * N_HEADS

    def page_scores(slot_bias):
        return [scores(*(r[0, :, pl.ds(c, N_HEADS, stride=2), :].reshape(n, hd) for c in range(2)), b)
                for r, b in zip(k_refs, slot_bias)]

    def update_pages(page_s):
        for s, r in zip(page_s, v_refs):
            update(s, r[0].reshape(n, vd))

    def finish():
        tok = lax.broadcasted_iota(jnp.int32, (tq, N_HEADS, nl), 0)
        qi = lax.broadcasted_iota(jnp.int32, (tq, N_HEADS, nl), 2) % tq
        sn = scores(*(kn_ref[0, pl.ds(c, tq * N_HEADS, stride=2), :] for c in range(2)))
        update(sn + jnp.where(tok <= qi, 0.0, NEG_INF), vn_ref[0])

        lam = _lambda(*lam_refs, lam_init)
        inv_l = jnp.broadcast_to(1.0 / l_ref[0:1], (nl, nl)).T
        o = acc_ref[0:half] * inv_l[0:half, 0:1] - lam * (acc_ref[half:nl] * inv_l[half:nl, 0:1])
        o = o * _rms_scale(o, SUBLN_EPS) * sg_ref[...] * (1.0 - lam_init)
        for h in range(N_HEADS):
            o_ref[0, :, h * vd:(h + 1) * vd] = o[h * tq:(h + 1) * tq].astype(bf16)

    return init, page_scores, update_pages, finish


def _ffn_attn_kernel(pt_ref, x_hbm, g_ref, wg_ref, wu_ref, wd_ref, fg_ref, q_ref, kn_ref, vn_ref,
                     lq1_ref, lk1_ref, lq2_ref, lk2_ref, sg_ref, *refs, pages, n_pages, tq, hd, final_norm):
    k_refs = refs[:pages]
    v_refs = refs[pages:2 * pages]
    y_ref, oa_ref, h_ref = refs[2 * pages:2 * pages + 3]
    attn_scratch, x_sem = refs[2 * pages + 3:-1], refs[-1]
    i, j = pl.program_id(0), pl.program_id(1)
    tm = y_ref.shape[0]
    init, page_scores, update_pages, finish = _paged_attn(
        q_ref, kn_ref, vn_ref, (lq1_ref, lk1_ref, lq2_ref, lk2_ref), sg_ref, k_refs, v_refs, oa_ref,
        attn_scratch, tq=tq, hd=hd)

    @pl.when(j == 0)
    def _():
        x_copy = pltpu.make_async_copy(x_hbm.at[pl.ds(pl.multiple_of(i * tm, tm), tm), :], y_ref, x_sem)
        x_copy.start()
        init()
        x_copy.wait()
        x = y_ref[...]
        h_ref[...] = (x * _rms_scale(x, RMS_EPS) * g_ref[...]).astype(bf16)

    scores = page_scores([jnp.where(j * pages + p < n_pages, 0.0, NEG_INF) for p in range(pages)])
    h = h_ref[...]
    gate = _dot(h, wg_ref[...])
    up = _dot(h, wu_ref[...])
    update_pages(scores)
    act = (gate * _sigmoid(gate) * up).astype(bf16)
    y_ref[...] += _dot(act, wd_ref[...])

    @pl.when(j == pl.num_programs(1) - 1)
    def _():
        finish()
        if final_norm:
            y = y_ref[...]
            y_ref[...] = y * _rms_scale(y, RMS_EPS) * fg_ref[...]


def _ffn_attn(x, g, wg, wu, wd, final_g, final_norm, page_table, seq0, q, k_new, v_new,
              cache_k, cache_v, lq1, lk1, lq2, lk2, sg, tm, tf):
    m, d = x.shape
    f = wg.shape[1]
    _, tq, _ = q.shape
    hd = d // N_HEADS // 2
    n_pages = page_table.shape[1]
    page = cache_k.shape[1]
    nl = 2 * N_HEADS * tq
    steps = f // tf
    pages = -(-n_pages // steps)
    nseq = m // tm
    assert nl == V7X_LANES and hd == V7X_LANES
    assert f % tf == 0 and m % tm == 0 and seq0 + nseq <= q.shape[0]

    def k_spec(p):
        return pl.BlockSpec((1, page, 2 * N_HEADS, hd),
                            lambda i, j, pt: (pt[seq0 + i, jnp.minimum(j * pages + p, n_pages - 1)], 0, 0, 0))

    def v_spec(p):
        return pl.BlockSpec((1, page, N_HEADS, 2 * hd),
                            lambda i, j, pt: (pt[seq0 + i, jnp.minimum(j * pages + p, n_pages - 1)], 0, 0, 0))

    vec = pl.BlockSpec((1, d), lambda i, j, pt: (0, 0))
    lvec = pl.BlockSpec((1, hd), lambda i, j, pt: (0, 0))
    grid_spec = pltpu.PrefetchScalarGridSpec(
        num_scalar_prefetch=1,
        grid=(nseq, steps),
        in_specs=[pl.BlockSpec(memory_space=pl.ANY), vec,
                  pl.BlockSpec((d, tf), lambda i, j, pt: (0, j)),
                  pl.BlockSpec((d, tf), lambda i, j, pt: (0, j)),
                  pl.BlockSpec((tf, d), lambda i, j, pt: (j, 0)),
                  vec,
                  pl.BlockSpec((1, tq, d), lambda i, j, pt: (seq0 + i, 0, 0)),
                  pl.BlockSpec((1, tq * 2 * N_HEADS, hd), lambda i, j, pt: (seq0 + i, 0, 0)),
                  pl.BlockSpec((1, tq * N_HEADS, 2 * hd), lambda i, j, pt: (seq0 + i, 0, 0)),
                  lvec, lvec, lvec, lvec,
                  pl.BlockSpec((1, 2 * hd), lambda i, j, pt: (0, 0))]
                 + [k_spec(p) for p in range(pages)] + [v_spec(p) for p in range(pages)],
        out_specs=[pl.BlockSpec((tm, d), lambda i, j, pt: (i, 0)),
                   pl.BlockSpec((1, tq, d), lambda i, j, pt: (i, 0, 0))],
        scratch_shapes=[pltpu.VMEM((tm, d), bf16),
                        pltpu.VMEM((2 * hd, nl), f32), pltpu.VMEM((N_HEADS, nl), f32),
                        pltpu.VMEM((V7X_SUBLANES, nl), f32), pltpu.VMEM((V7X_SUBLANES, nl), f32),
                        pltpu.VMEM((nl, 2 * hd), f32),
                        pltpu.SemaphoreType.DMA(())],
    )
    return pl.pallas_call(
        functools.partial(_ffn_attn_kernel, pages=pages, n_pages=n_pages, tq=tq, hd=hd,
                          final_norm=final_norm),
        grid_spec=grid_spec,
        out_shape=[jax.ShapeDtypeStruct((m, d), f32), jax.ShapeDtypeStruct((nseq, tq, d), bf16)],
        compiler_params=_params("parallel", "arbitrary"),
        name="ffn_final_paged_attn" if final_norm else "ffn_paged_attn",
    )(page_table, x, g, wg, wu, wd, final_g, q, k_new, v_new, lq1, lk1, lq2, lk2, sg,
      *([cache_k] * pages), *([cache_v] * pages))


def _oproj_kernel(o_ref, w_ref, x_ref, y_ref):
    y_ref[...] = x_ref[...] + _dot(o_ref[...], w_ref[...])


def _oproj(o, w, x, tm, tn):
    m, d = x.shape
    return pl.pallas_call(
        _oproj_kernel,
        grid=(m // tm, d // tn),
        in_specs=[pl.BlockSpec((tm, o.shape[1]), lambda i, j: (i, 0)),
                  pl.BlockSpec((o.shape[1], tn), lambda i, j: (0, j)),
                  pl.BlockSpec((tm, tn), lambda i, j: (i, j))],
        out_specs=pl.BlockSpec((tm, tn), lambda i, j: (i, j)),
        out_shape=jax.ShapeDtypeStruct((m, d), f32),
        compiler_params=_params("parallel", "arbitrary"),
        name="attn_out_proj",
    )(o, w, x)


def _conv_layer(x, state, p, tm, conv_tm):
    bsz, t, d = x.shape
    u = _pw1_glu(x.reshape(bsz * t, d), p['attn_norm_g'][0:1], p['conv_w_pw1'], p['conv_b_pw1'],
                 tm, min(512, d)).reshape(bsz, t, d)
    state_padded = jnp.pad(state, ((0, 0), (HIST_ROWS - (CONV_WIDTH - 1), 0), (0, 0)))
    x1 = _conv_pw2(u, state_padded, x, p['conv_w_dw'], p['conv_b_dw'], p['conv_ln_g'],
                   p['conv_ln_b'], p['conv_w_pw2'], p['conv_b_pw2'], conv_tm)
    return x1.reshape(bsz * t, d), jnp.concatenate([state, u], axis=1)[:, t:][None]


def _project_qkv(x1, p, tm):
    d = x1.shape[1]
    hd = d // N_HEADS // 2
    return _qkv(x1, p['kv_norm_g'], p['attn_norm_g'][1:2], p['w_k'], p['w_v'], p['attn_w_q'],
                tm, min(256, d), hd ** -0.5 * math.log2(math.e))


def kernel(x_prompt, x_sample, state_conv, cache_k, cache_v, page_table, attn_norm_g, ffn_norm_g,
           conv_w_pw1, conv_b_pw1, conv_w_dw, conv_b_dw, conv_ln_g, conv_ln_b, conv_w_pw2,
           conv_b_pw2, kv_norm_g, w_k, w_v, attn_w_q, attn_lambda_q1, attn_lambda_k1,
           attn_lambda_q2, attn_lambda_k2, attn_subln_g, attn_w_o, ffn_w_gate, ffn_w_up,
           ffn_w_down, final_norm_g):
    assert attn_norm_g.shape[0] == 2 and conv_w_pw1.shape[0] == 1 and attn_w_q.shape[0] == 1
    d = x_prompt.shape[-1]
    p = dict(
        attn_norm_g=attn_norm_g, ffn_norm_g=ffn_norm_g,
        conv_w_pw1=conv_w_pw1[0].astype(bf16), conv_b_pw1=conv_b_pw1,
        conv_w_dw=jnp.broadcast_to(conv_w_dw[0][:, None, :], (CONV_WIDTH, V7X_SUBLANES, d)),
        conv_b_dw=conv_b_dw, conv_ln_g=conv_ln_g, conv_ln_b=conv_ln_b,
        conv_w_pw2=conv_w_pw2[0].astype(bf16), conv_b_pw2=conv_b_pw2,
        kv_norm_g=kv_norm_g.reshape(1, d), w_k=w_k.astype(bf16), w_v=w_v.astype(bf16),
        attn_w_q=attn_w_q[0].astype(bf16), attn_w_o=attn_w_o[0].astype(bf16),
        final_norm_g=final_norm_g.reshape(1, d),
    )
    lams = (attn_lambda_q1, attn_lambda_k1, attn_lambda_q2, attn_lambda_k2, attn_subln_g)
    hd = d // N_HEADS // 2
    bsz, t, _ = x_prompt.shape
    sb, st, _ = x_sample.shape
    mp, ms = bsz * t, sb * st
    tm = PROMPT_ROW_TILE
    assert mp % tm == 0 and 2 * (mp // tm) == sb

    x1_s, conv_s = _conv_layer(x_sample, state_conv[0], p, ms, st)
    x1_s, *w0 = _ffn(x1_s, ffn_norm_g[0:1], ffn_w_gate, ffn_w_up, ffn_w_down, p['final_norm_g'],
                     ms, FFN_TILE, False, layer=0)
    w1 = (ffn_w_gate[1].astype(bf16), ffn_w_up[1].astype(bf16), ffn_w_down[1].astype(bf16))
    k_s, v_s, q_s, _, _ = _project_qkv(x1_s, p, ms)
    paged = (q_s.reshape(sb, st, d), k_s.reshape(sb, st * 2 * N_HEADS, hd),
             v_s.reshape(sb, st * N_HEADS, 2 * hd), cache_k, cache_v, *lams)

    def ffn_attn(x, layer, w, final_norm, seq0):
        return _ffn_attn(x, ffn_norm_g[layer:layer + 1], *w, p['final_norm_g'], final_norm,
                         page_table, seq0, *paged, tm=tm, tf=FUSED_FFN_TILE)

    conv0 = jnp.zeros((bsz, CONV_WIDTH - 1, d), x_prompt.dtype)
    x1_p, conv_p = _conv_layer(x_prompt, conv0, p, tm, 512)
    x1_p, o_s0 = ffn_attn(x1_p, 0, w0, False, 0)
    k_p, v_p, q_p, kb_p, vb_p = _project_qkv(x1_p, p, tm)
    o_p = _prompt_attn(q_p.reshape(bsz, t, d), kb_p.reshape(bsz, t, d), vb_p.reshape(bsz, t, d),
                       *lams, tq=512)
    x2_p = _oproj(o_p.reshape(mp, d), p['attn_w_o'], x1_p, tm, min(512, d))
    y_p, o_s1 = ffn_attn(x2_p, 1, w1, True, sb // 2)

    o_s = jnp.concatenate([o_s0, o_s1], axis=0).reshape(ms, d)
    x2_s = _oproj(o_s, p['attn_w_o'], x1_s, ms, min(512, d))
    y_s = _ffn(x2_s, ffn_norm_g[1:2], *w1, p['final_norm_g'], ms, FFN_TILE, True)

    def heads(k, v, b, tt):
        return k.reshape(b, tt, 2 * N_HEADS, hd), v.reshape(b, tt, N_HEADS, 2 * hd)

    return (y_p.reshape(bsz, t, d), y_s.reshape(sb, st, d), conv_p, conv_s,
            *heads(k_p, v_p, bsz, t), *heads(k_s, v_s, sb, st))
```

```python
import functools
import math

import jax
import jax.numpy as jnp
from jax import lax
from jax.experimental import pallas as pl
from jax.experimental.pallas import tpu as pltpu

N_HEADS = 8
CONV_WIDTH = 31
RMS_EPS = 1e-6
SUBLN_EPS = 1e-5
LN_EPS = 1e-5
NEG_INF = -1e30
B_LAYER_INDEX = 1

V7X_VMEM_BYTES = 64 * 1024 * 1024
V7X_SUBLANES = 8
V7X_LANES = 128
VMEM_LIMIT_BYTES = V7X_VMEM_BYTES - 8 * 1024 * 1024

HIST_ROWS = 32
CONV_ROW_CHUNK = 32
CONV_COL_CHUNK = 256
CONV_GROUP_ROWS = 128
CONV_MAX_ROWS = 1024
ATTN_KEY_SPLIT = 2
FFN_TILE = 512
PROMPT_ROW_TILE = 1024
FUSED_FFN_TILE = 256

f32 = jnp.float32
bf16 = jnp.bfloat16


def _params(*sem):
    return pltpu.CompilerParams(dimension_semantics=sem, vmem_limit_bytes=VMEM_LIMIT_BYTES)


def _rms_scale(x, eps):
    return lax.rsqrt(jnp.mean(x * x, axis=-1, keepdims=True) + eps)


def _sigmoid(x):
    return 1.0 / (1.0 + jnp.exp(-x))


def _aligned(i, n):
    return i if isinstance(i, int) else pl.multiple_of(i, n)


def _dot(a, b):
    return jnp.dot(a, b, preferred_element_type=f32)


def _pw1_glu_kernel(x_ref, g_ref, wa_ref, wb_ref, ba_ref, bb_ref, u_ref, h_ref):
    @pl.when(pl.program_id(1) == 0)
    def _():
        x = x_ref[...]
        h_ref[...] = (x * _rms_scale(x, RMS_EPS) * g_ref[...]).astype(bf16)

    h = h_ref[...]
    a = _dot(h, wa_ref[...]) + ba_ref[...]
    b = _dot(h, wb_ref[...]) + bb_ref[...]
    u_ref[...] = a * _sigmoid(b)


def _pw1_glu(x, g, w, b, tm, tn):
    m, d = x.shape
    nj = d // tn
    return pl.pallas_call(
        _pw1_glu_kernel,
        grid=(m // tm, nj),
        in_specs=[
            pl.BlockSpec((tm, d), lambda i, j: (i, 0)),
            pl.BlockSpec((1, d), lambda i, j: (0, 0)),
            pl.BlockSpec((d, tn), lambda i, j: (0, j)),
            pl.BlockSpec((d, tn), lambda i, j: (0, j + nj)),
            pl.BlockSpec((1, tn), lambda i, j: (0, j)),
            pl.BlockSpec((1, tn), lambda i, j: (0, j + nj)),
        ],
        out_specs=pl.BlockSpec((tm, tn), lambda i, j: (i, j)),
        out_shape=jax.ShapeDtypeStruct((m, d), f32),
        scratch_shapes=[pltpu.VMEM((tm, d), bf16)],
        compiler_params=_params("parallel", "arbitrary"),
        name="pw1_glu",
    )(x, g, w, w, b, b)


def _conv_pw2_kernel(u_ref, uprev_ref, st_ref, x_ref, wdw_ref, bdw_ref, lng_ref, lnb_ref,
                     w2_ref, b2_ref, o_ref, full_ref, s_ref, *, tm, rc, bb):
    i = pl.program_id(1)
    first_tap = HIST_ROWS - (CONV_WIDTH - 1)
    d = u_ref.shape[2]
    cc = min(CONV_COL_CHUNK, d)

    taps_by_shift = [[(a, 8 * a + r - first_tap) for a in range(HIST_ROWS // 8 + 1)
                      if 0 <= 8 * a + r - first_tap < CONV_WIDTH] for r in range(8)]

    def fill_history(bi):
        hist = st_ref[bi] if uprev_ref is None else jnp.where(i == 0, st_ref[bi], uprev_ref[bi])
        full_ref[0:HIST_ROWS, :] = hist
        full_ref[HIST_ROWS:HIST_ROWS + tm, :] = u_ref[bi]

    def conv_chunk(bi, r0):
        pieces = []
        for c0 in range(0, d, cc):
            piece = jnp.broadcast_to(bdw_ref[:, c0:c0 + cc], (rc, cc))
            for r, taps in enumerate(taps_by_shift):
                rows = rc + (8 if r else 0)
                g = None
                for a, w in taps:
                    win = full_ref[pl.ds(_aligned(r0 + 8 * a, 8), rows), c0:c0 + cc]
                    term = win.reshape(rows // 8, 8, cc) * wdw_ref[w, :, c0:c0 + cc][None]
                    g = term if g is None else g + term
                piece = piece + g.reshape(rows, cc)[r:r + rc]
            pieces.append(piece)
        acc = jnp.concatenate(pieces, axis=1)
        mu = jnp.mean(acc, axis=-1, keepdims=True)
        cen = acc - mu
        var = jnp.mean(cen * cen, axis=-1, keepdims=True)
        y = cen * lax.rsqrt(var + LN_EPS) * lng_ref[...] + lnb_ref[...]
        s_ref[pl.ds(_aligned(bi * tm + r0, rc), rc), :] = (y * _sigmoid(y)).astype(s_ref.dtype)

    def one_sequence(bi, carry):
        fill_history(bi)
        lax.fori_loop(0, tm // rc, lambda c, carry: (conv_chunk(bi, c * rc), carry)[1], 0)
        return carry

    if bb > 1:
        lax.fori_loop(0, bb, one_sequence, 0)
        y = _dot(s_ref[...].astype(bf16), w2_ref[...]) + b2_ref[...]
        o_ref[...] = x_ref[...] + y.reshape(bb, tm, d)
        return

    fill_history(0)
    groups = max(tm // CONV_GROUP_ROWS, 1)
    rows = tm // groups

    def pointwise(g):
        sl = slice(g * rows, (g + 1) * rows)
        o_ref[0, sl, :] = x_ref[0, sl, :] + _dot(s_ref[sl, :].astype(bf16), w2_ref[...]) + b2_ref[...]

    for g in range(groups):
        if g:
            pointwise(g - 1)
        for c in range(g * rows // rc, (g + 1) * rows // rc):
            conv_chunk(0, c * rc)
    pointwise(groups - 1)


def _conv_pw2_no_prev_kernel(u_ref, st_ref, *rest, **kw):
    _conv_pw2_kernel(u_ref, None, st_ref, *rest, **kw)


def _conv_pw2(u, state_padded, x, wdw, bdw, lng, lnb, w2, b2, tm):
    bsz, t, d = u.shape
    rc = min(CONV_ROW_CHUNK, tm)
    single_tile = t == tm
    bb = bsz if (single_tile and bsz * tm <= CONV_MAX_ROWS) else 1
    tile = pl.BlockSpec((bb, tm, d), lambda b, i: (b, i, 0))
    st_spec = pl.BlockSpec((bb, HIST_ROWS, d), lambda b, i: (b, 0, 0))
    vec = pl.BlockSpec((1, d), lambda b, i: (0, 0))
    if single_tile:
        body, prev_specs, prev_args = _conv_pw2_no_prev_kernel, [], []
    else:
        hist_blocks_per_tile = tm // HIST_ROWS
        body = _conv_pw2_kernel
        prev_specs = [pl.BlockSpec(
            (bb, HIST_ROWS, d), lambda b, i: (b, jnp.maximum(i * hist_blocks_per_tile - 1, 0), 0))]
        prev_args = [u]
    s_dtype = bf16 if rc % 16 == 0 else f32
    return pl.pallas_call(
        functools.partial(body, tm=tm, rc=rc, bb=bb),
        grid=(bsz // bb, t // tm),
        in_specs=[tile] + prev_specs + [
            st_spec,
            tile,
            pl.BlockSpec((CONV_WIDTH, V7X_SUBLANES, d), lambda b, i: (0, 0, 0)),
            vec, vec, vec,
            pl.BlockSpec((d, d), lambda b, i: (0, 0), pipeline_mode=pl.Buffered(1)),
            vec,
        ],
        out_specs=tile,
        out_shape=jax.ShapeDtypeStruct((bsz, t, d), f32),
        scratch_shapes=[pltpu.VMEM((HIST_ROWS + tm, d), f32), pltpu.VMEM((bb * tm, d), s_dtype)],
        compiler_params=_params("parallel", "arbitrary"),
        name="conv_pw2",
    )(u, *prev_args, state_padded, x, wdw, bdw, lng, lnb, w2, b2)


def _ffn_kernel(x_ref, g_ref, wg_ref, wu_ref, wd_ref, fg_ref, o_ref, *refs, final_norm, emit_bf16):
    h_ref = refs[-1]
    j = pl.program_id(1)

    @pl.when(j == 0)
    def _():
        x = x_ref[...]
        h_ref[...] = (x * _rms_scale(x, RMS_EPS) * g_ref[...]).astype(bf16)
        o_ref[...] = x

    wg, wu, wd = wg_ref[...], wu_ref[...], wd_ref[...]
    if emit_bf16:
        wg, wu, wd = wg.astype(bf16), wu.astype(bf16), wd.astype(bf16)
        for out_ref, w in zip(refs[:3], (wg, wu, wd)):
            out_ref[...] = w
    h = h_ref[...]
    gate = _dot(h, wg)
    up = _dot(h, wu)
    act = (gate * _sigmoid(gate) * up).astype(bf16)
    o_ref[...] += _dot(act, wd)

    if final_norm:
        @pl.when(j == pl.num_programs(1) - 1)
        def _():
            y = o_ref[...]
            o_ref[...] = y * _rms_scale(y, RMS_EPS) * fg_ref[...]


def _ffn(x, g, wg, wu, wd, final_g, tm, tf, final_norm, layer=None):
    m, d = x.shape
    emit = layer is not None
    f = wg.shape[-1]
    if emit:
        assert m == tm
        w_in = [pl.BlockSpec((None, d, tf), lambda i, j: (layer, 0, j)),
                pl.BlockSpec((None, d, tf), lambda i, j: (layer, 0, j)),
                pl.BlockSpec((None, tf, d), lambda i, j: (layer, j, 0))]
    else:
        w_in = [pl.BlockSpec((d, tf), lambda i, j: (0, j)),
                pl.BlockSpec((d, tf), lambda i, j: (0, j)),
                pl.BlockSpec((tf, d), lambda i, j: (j, 0))]
    w_out = [pl.BlockSpec((d, tf), lambda i, j: (0, j)),
             pl.BlockSpec((d, tf), lambda i, j: (0, j)),
             pl.BlockSpec((tf, d), lambda i, j: (j, 0))] if emit else []
    w_out_shape = [jax.ShapeDtypeStruct((d, f), bf16), jax.ShapeDtypeStruct((d, f), bf16),
                   jax.ShapeDtypeStruct((f, d), bf16)] if emit else []
    out = pl.pallas_call(
        functools.partial(_ffn_kernel, final_norm=final_norm, emit_bf16=emit),
        grid=(m // tm, f // tf),
        in_specs=[
            pl.BlockSpec((tm, d), lambda i, j: (i, 0), pipeline_mode=pl.Buffered(1)),
            pl.BlockSpec((1, d), lambda i, j: (0, 0)),
            *w_in,
            pl.BlockSpec((1, d), lambda i, j: (0, 0)),
        ],
        out_specs=[pl.BlockSpec((tm, d), lambda i, j: (i, 0))] + w_out,
        out_shape=[jax.ShapeDtypeStruct((m, d), f32)] + w_out_shape,
        scratch_shapes=[pltpu.VMEM((tm, d), bf16)],
        compiler_params=_params("parallel", "arbitrary"),
        name="ffn_final" if final_norm else "ffn",
    )(x, g, wg, wu, wd, final_g)
    return out if emit else out[0]


def _qkv_kernel(x_ref, gkv_ref, gq_ref, wk_ref, wv_ref, wq_ref,
                k_ref, v_ref, q_ref, kb_ref, vb_ref, hkv_ref, hq_ref, *, q_scale):
    @pl.when(pl.program_id(1) == 0)
    def _():
        x = x_ref[...]
        xn = x * _rms_scale(x, RMS_EPS)
        hkv_ref[...] = (xn * gkv_ref[...]).astype(bf16)
        hq_ref[...] = (xn * gq_ref[...]).astype(bf16)

    hkv = hkv_ref[...]
    k = _dot(hkv, wk_ref[...])
    v = _dot(hkv, wv_ref[...])
    k_ref[...] = k
    v_ref[...] = v
    kb_ref[...] = k.astype(bf16)
    vb_ref[...] = v.astype(bf16)
    q_ref[...] = (_dot(hq_ref[...], wq_ref[...]) * q_scale).astype(bf16)


def _qkv(x, gkv, gq, wk, wv, wq, tm, tn, q_scale):
    m, d = x.shape
    wspec = pl.BlockSpec((d, tn), lambda i, j: (0, j))
    ospec = pl.BlockSpec((tm, tn), lambda i, j: (i, j))
    vec = pl.BlockSpec((1, d), lambda i, j: (0, 0))
    return pl.pallas_call(
        functools.partial(_qkv_kernel, q_scale=q_scale),
        grid=(m // tm, d // tn),
        in_specs=[pl.BlockSpec((tm, d), lambda i, j: (i, 0), pipeline_mode=pl.Buffered(1)),
                  vec, vec, wspec, wspec, wspec],
        out_specs=[ospec] * 5,
        out_shape=[jax.ShapeDtypeStruct((m, d), f32), jax.ShapeDtypeStruct((m, d), f32),
                   jax.ShapeDtypeStruct((m, d), bf16), jax.ShapeDtypeStruct((m, d), bf16),
                   jax.ShapeDtypeStruct((m, d), bf16)],
        scratch_shapes=[pltpu.VMEM((tm, d), bf16), pltpu.VMEM((tm, d), bf16)],
        compiler_params=_params("parallel", "arbitrary"),
        name="qkv_proj",
    )(x, gkv, gq, wk, wv, wq)


def _lambda(lam_ref, lam_init):
    e1 = jnp.exp(jnp.sum(lam_ref[0:1, :] * lam_ref[1:2, :], axis=-1, keepdims=True))
    e2 = jnp.exp(jnp.sum(lam_ref[2:3, :] * lam_ref[3:4, :], axis=-1, keepdims=True))
    return e1 - e2 + lam_init


def _lam_init():
    return 0.8 - 0.6 * math.exp(-0.3 * B_LAYER_INDEX)


def _prompt_attn_kernel(q_ref, k_ref, v_ref, lam_ref, sg_ref, o_ref,
                        vt_ref, qt_ref, m_ref, l_ref, acc_ref, *, tq, hd):
    i = pl.program_id(2)
    t = k_ref.shape[1]
    lam_init = _lam_init()
    lam = _lambda(lam_ref, lam_init)

    @pl.when(i == 0)
    def _():
        for r0 in range(0, t, tq):
            vt_ref[:, r0:r0 + tq] = v_ref[0, r0:r0 + tq, :].astype(f32).T.astype(bf16)

    for c in range(2):
        qt_ref[c] = q_ref[0, :, c * hd:(c + 1) * hd].astype(f32).T.astype(bf16)
    m_ref[...] = jnp.full(m_ref.shape, NEG_INF, f32)
    l_ref[...] = jnp.zeros(l_ref.shape, f32)
    acc_ref[...] = jnp.zeros(acc_ref.shape, f32)

    def block(j, masked):
        r0 = pl.multiple_of(j * tq, tq)
        tk = tq // ATTN_KEY_SPLIT
        kblk = k_ref[0, pl.ds(r0, tq), :]
        vtblk = vt_ref[:, pl.ds(r0, tq)]
        s = [[_dot(kblk[a * tk:(a + 1) * tk, c * hd:(c + 1) * hd], qt_ref[c]) for c in range(2)]
             for a in range(ATTN_KEY_SPLIT)]
        for a in range(ATTN_KEY_SPLIT):
            if masked:
                key = lax.broadcasted_iota(jnp.int32, (tk, tq), 0) + a * tk
                query = lax.broadcasted_iota(jnp.int32, (tk, tq), 1)
                keep = key <= query
            for c in range(2):
                sc = jnp.where(keep, s[a][c], NEG_INF) if masked else s[a][c]
                m_old = m_ref[c]
                m_new = jnp.maximum(m_old, jnp.max(sc, axis=0, keepdims=True))
                alpha = jnp.exp2(m_old - m_new)
                p = jnp.exp2(sc - m_new[0:1])
                l_ref[c] = alpha * l_ref[c] + jnp.sum(p, axis=0, keepdims=True)
                acc_ref[c] = alpha[0:1] * acc_ref[c] + _dot(vtblk[:, a * tk:(a + 1) * tk], p.astype(bf16))
                m_ref[c] = m_new

    def body(j, carry):
        block(j, masked=False)
        return carry

    lax.fori_loop(0, i, body, 0)
    block(i, masked=True)

    ot = acc_ref[0] * (1.0 / l_ref[0][0:1]) - lam * (acc_ref[1] * (1.0 / l_ref[1][0:1]))
    o = ot.T
    o = o * _rms_scale(o, SUBLN_EPS) * sg_ref[...] * (1.0 - lam_init)
    o_ref[0] = o.astype(bf16)


def _prompt_attn(q, k, v, lam4, sg, tq):
    bsz, t, d = q.shape
    hd = d // N_HEADS // 2
    vd = 2 * hd
    return pl.pallas_call(
        functools.partial(_prompt_attn_kernel, tq=tq, hd=hd),
        grid=(bsz, N_HEADS, t // tq),
        in_specs=[
            pl.BlockSpec((1, tq, vd), lambda b, h, i: (b, i, h)),
            pl.BlockSpec((1, t, vd), lambda b, h, i: (b, 0, h)),
            pl.BlockSpec((1, t, vd), lambda b, h, i: (b, 0, h)),
            pl.BlockSpec((4, hd), lambda b, h, i: (0, 0)),
            pl.BlockSpec((1, vd), lambda b, h, i: (0, 0)),
        ],
        out_specs=pl.BlockSpec((1, tq, vd), lambda b, h, i: (b, i, h)),
        out_shape=jax.ShapeDtypeStruct((bsz, t, d), bf16),
        scratch_shapes=[pltpu.VMEM((vd, t), bf16), pltpu.VMEM((2, hd, tq), bf16),
                        pltpu.VMEM((2, V7X_SUBLANES, tq), f32), pltpu.VMEM((2, V7X_SUBLANES, tq), f32),
                        pltpu.VMEM((2, vd, tq), f32)],
        compiler_params=_params("parallel", "parallel", "arbitrary"),
        name="prompt_diff_attn",
    )(q, k, v, lam4, sg)


def _paged_attn(q_ref, kn_ref, vn_ref, lam_ref, sg_ref, k_refs, v_refs, o_ref, scratch, *, tq, hd):
    qbd_ref, bias_ref, m_ref, l_ref, acc_ref = scratch
    vd = 2 * hd
    nl = 2 * N_HEADS * tq
    half = N_HEADS * tq
    page = k_refs[0].shape[1]
    lam_init = _lam_init()

    def init():
        qf = q_ref[0].astype(f32)
        zero = jnp.zeros((hd, half), f32)
        halves = []
        for c in range(2):
            qc = jnp.concatenate([qf[:, (2 * h + c) * hd:(2 * h + c + 1) * hd] for h in range(N_HEADS)], axis=0).T
            halves.append(jnp.concatenate([qc, zero] if c == 0 else [zero, qc], axis=1))
        qbd_ref[...] = jnp.concatenate(halves, axis=0)
        head = lax.broadcasted_iota(jnp.int32, (N_HEADS, nl), 0)
        lane = lax.broadcasted_iota(jnp.int32, (N_HEADS, nl), 1)
        bias_ref[...] = jnp.where((lane % half) // tq == head, 0.0, NEG_INF)
        m_ref[...] = jnp.full(m_ref.shape, NEG_INF, f32)
        l_ref[...] = jnp.zeros(l_ref.shape, f32)
        acc_ref[...] = jnp.zeros(acc_ref.shape, f32)

    def scores(k0, k1, slot_bias=0.0):
        s = _dot(jnp.concatenate([k0, k1], axis=1), qbd_ref[...])
        return s.reshape(s.shape[0] // N_HEADS, N_HEADS, nl) + (bias_ref[...] + slot_bias)[None]

    def update(s, v2):
        m_old = m_ref[...]
        m_new = jnp.maximum(m_old, jnp.max(jnp.max(s, axis=0), axis=0, keepdims=True))
        alpha = jnp.exp2(m_old - m_new)
        p = jnp.exp2(s - m_new[None])
        l_ref[...] = alpha * l_ref[...] + jnp.sum(jnp.sum(p, axis=0), axis=0, keepdims=True)
        m_ref[...] = m_new
        pv = lax.dot_general(p.reshape(v2.shape[0], nl).astype(bf16), v2.astype(bf16),
                             (((0,), (0,)), ((), ())), preferred_element_type=f32)
        alpha_col = jnp.broadcast_to(alpha[0:1], (nl, nl)).T
        acc_ref[...] = jnp.concatenate([alpha_col] * (vd // nl), axis=1) * acc_ref[...] + pv

    n = page * N_HEADS

    def page_scores(slot_bias):
        return [scores(*(r[0, :, pl.ds(c, N_HEADS, stride=2), :].reshape(n, hd) for c in range(2)), b)
                for r, b in zip(k_refs, slot_bias)]

    def update_pages(page_s):
        for s, r in zip(page_s, v_refs):
            update(s, r[0].reshape(n, vd))

    def finish():
        tok = lax.broadcasted_iota(jnp.int32, (tq, N_HEADS, nl), 0)
        qi = lax.broadcasted_iota(jnp.int32, (tq, N_HEADS, nl), 2) % tq
        sn = scores(*(kn_ref[0, pl.ds(c, tq * N_HEADS, stride=2), :] for c in range(2)))
        update(sn + jnp.where(tok <= qi, 0.0, NEG_INF), vn_ref[0])

        lam = _lambda(lam_ref, lam_init)
        inv_l = jnp.broadcast_to(1.0 / l_ref[0:1], (nl, nl)).T
        o = acc_ref[0:half] * inv_l[0:half, 0:1] - lam * (acc_ref[half:nl] * inv_l[half:nl, 0:1])
        o = o * _rms_scale(o, SUBLN_EPS) * sg_ref[...] * (1.0 - lam_init)
        for h in range(N_HEADS):
            o_ref[0, :, h * vd:(h + 1) * vd] = o[h * tq:(h + 1) * tq].astype(bf16)

    return init, page_scores, update_pages, finish


def _ffn_attn_kernel(pt_ref, x_hbm, g_ref, wg_ref, wu_ref, wd_ref, q_ref, kn_ref, vn_ref,
                     lam_ref, sg_ref, *refs, pages, n_pages, tq, hd, final_norm):
    k_refs = refs[:pages]
    v_refs = refs[pages:2 * pages]
    y_ref, oa_ref, h_ref = refs[2 * pages:2 * pages + 3]
    attn_scratch, x_sem = refs[2 * pages + 3:-1], refs[-1]
    i, j = pl.program_id(0), pl.program_id(1)
    tm = y_ref.shape[0]
    init, page_scores, update_pages, finish = _paged_attn(
        q_ref, kn_ref, vn_ref, lam_ref, sg_ref, k_refs, v_refs, oa_ref,
        attn_scratch, tq=tq, hd=hd)

    @pl.when(j == 0)
    def _():
        x_copy = pltpu.make_async_copy(x_hbm.at[pl.ds(pl.multiple_of(i * tm, tm), tm), :], y_ref, x_sem)
        x_copy.start()
        init()
        x_copy.wait()
        x = y_ref[...]
        h_ref[...] = (x * _rms_scale(x, RMS_EPS) * g_ref[0:1, :]).astype(bf16)

    scores = page_scores([jnp.where(j * pages + p < n_pages, 0.0, NEG_INF) for p in range(pages)])
    h = h_ref[...]
    gate = _dot(h, wg_ref[...])
    up = _dot(h, wu_ref[...])
    update_pages(scores)
    act = (gate * _sigmoid(gate) * up).astype(bf16)
    y_ref[...] += _dot(act, wd_ref[...])

    @pl.when(j == pl.num_programs(1) - 1)
    def _():
        finish()
        if final_norm:
            y = y_ref[...]
            y_ref[...] = y * _rms_scale(y, RMS_EPS) * g_ref[1:2, :]


def _ffn_attn(x, gains, wg, wu, wd, final_norm, page_table, seq0, q, k_new, v_new,
              cache_k, cache_v, lam4, sg, tm, tf):
    m, d = x.shape
    f = wg.shape[1]
    _, tq, _ = q.shape
    hd = d // N_HEADS // 2
    n_pages = page_table.shape[1]
    page = cache_k.shape[1]
    nl = 2 * N_HEADS * tq
    steps = f // tf
    pages = -(-n_pages // steps)
    nseq = m // tm
    assert nl == V7X_LANES and hd == V7X_LANES
    assert f % tf == 0 and m % tm == 0 and seq0 + nseq <= q.shape[0]

    def k_spec(p):
        return pl.BlockSpec((1, page, 2 * N_HEADS, hd),
                            lambda i, j, pt: (pt[seq0 + i, jnp.minimum(j * pages + p, n_pages - 1)], 0, 0, 0))

    def v_spec(p):
        return pl.BlockSpec((1, page, N_HEADS, 2 * hd),
                            lambda i, j, pt: (pt[seq0 + i, jnp.minimum(j * pages + p, n_pages - 1)], 0, 0, 0))

    grid_spec = pltpu.PrefetchScalarGridSpec(
        num_scalar_prefetch=1,
        grid=(nseq, steps),
        in_specs=[pl.BlockSpec(memory_space=pl.ANY),
                  pl.BlockSpec((2, d), lambda i, j, pt: (0, 0)),
                  pl.BlockSpec((d, tf), lambda i, j, pt: (0, j)),
                  pl.BlockSpec((d, tf), lambda i, j, pt: (0, j)),
                  pl.BlockSpec((tf, d), lambda i, j, pt: (j, 0)),
                  pl.BlockSpec((1, tq, d), lambda i, j, pt: (seq0 + i, 0, 0)),
                  pl.BlockSpec((1, tq * 2 * N_HEADS, hd), lambda i, j, pt: (seq0 + i, 0, 0)),
                  pl.BlockSpec((1, tq * N_HEADS, 2 * hd), lambda i, j, pt: (seq0 + i, 0, 0)),
                  pl.BlockSpec((4, hd), lambda i, j, pt: (0, 0)),
                  pl.BlockSpec((1, 2 * hd), lambda i, j, pt: (0, 0))]
                 + [k_spec(p) for p in range(pages)] + [v_spec(p) for p in range(pages)],
        out_specs=[pl.BlockSpec((tm, d), lambda i, j, pt: (i, 0)),
                   pl.BlockSpec((1, tq, d), lambda i, j, pt: (i, 0, 0))],
        scratch_shapes=[pltpu.VMEM((tm, d), bf16),
                        pltpu.VMEM((2 * hd, nl), f32), pltpu.VMEM((N_HEADS, nl), f32),
                        pltpu.VMEM((V7X_SUBLANES, nl), f32), pltpu.VMEM((V7X_SUBLANES, nl), f32),
                        pltpu.VMEM((nl, 2 * hd), f32),
                        pltpu.SemaphoreType.DMA(())],
    )
    return pl.pallas_call(
        functools.partial(_ffn_attn_kernel, pages=pages, n_pages=n_pages, tq=tq, hd=hd,
                          final_norm=final_norm),
        grid_spec=grid_spec,
        out_shape=[jax.ShapeDtypeStruct((m, d), f32), jax.ShapeDtypeStruct((nseq, tq, d), bf16)],
        compiler_params=_params("parallel", "arbitrary"),
        name="ffn_final_paged_attn" if final_norm else "ffn_paged_attn",
    )(page_table, x, gains, wg, wu, wd, q, k_new, v_new, lam4, sg,
      *([cache_k] * pages), *([cache_v] * pages))


def _oproj_kernel(o_ref, w_ref, x_ref, y_ref):
    y_ref[...] = x_ref[...] + _dot(o_ref[...], w_ref[...])


def _oproj(o, w, x, tm, tn):
    m, d = x.shape
    return pl.pallas_call(
        _oproj_kernel,
        grid=(m // tm, d // tn),
        in_specs=[pl.BlockSpec((tm, o.shape[1]), lambda i, j: (i, 0)),
                  pl.BlockSpec((o.shape[1], tn), lambda i, j: (0, j)),
                  pl.BlockSpec((tm, tn), lambda i, j: (i, j))],
        out_specs=pl.BlockSpec((tm, tn), lambda i, j: (i, j)),
        out_shape=jax.ShapeDtypeStruct((m, d), f32),
        compiler_params=_params("parallel", "arbitrary"),
        name="attn_out_proj",
    )(o, w, x)


def _conv_layer(x, state, p, tm, conv_tm):
    bsz, t, d = x.shape
    u = _pw1_glu(x.reshape(bsz * t, d), p['attn_norm_g'][0:1], p['conv_w_pw1'], p['conv_b_pw1'],
                 tm, min(512, d)).reshape(bsz, t, d)
    state_padded = jnp.pad(state, ((0, 0), (HIST_ROWS - (CONV_WIDTH - 1), 0), (0, 0)))
    x1 = _conv_pw2(u, state_padded, x, p['conv_w_dw'], p['conv_b_dw'], p['conv_ln_g'],
                   p['conv_ln_b'], p['conv_w_pw2'], p['conv_b_pw2'], conv_tm)
    return x1.reshape(bsz * t, d), jnp.concatenate([state, u], axis=1)[:, t:][None]


def _project_qkv(x1, p, tm):
    d = x1.shape[1]
    hd = d // N_HEADS // 2
    return _qkv(x1, p['kv_norm_g'], p['attn_norm_g'][1:2], p['w_k'], p['w_v'], p['attn_w_q'],
                tm, min(256, d), hd ** -0.5 * math.log2(math.e))


def kernel(x_prompt, x_sample, state_conv, cache_k, cache_v, page_table, attn_norm_g, ffn_norm_g,
           conv_w_pw1, conv_b_pw1, conv_w_dw, conv_b_dw, conv_ln_g, conv_ln_b, conv_w_pw2,
           conv_b_pw2, kv_norm_g, w_k, w_v, attn_w_q, attn_lambda_q1, attn_lambda_k1,
           attn_lambda_q2, attn_lambda_k2, attn_subln_g, attn_w_o, ffn_w_gate, ffn_w_up,
           ffn_w_down, final_norm_g):
    assert attn_norm_g.shape[0] == 2 and conv_w_pw1.shape[0] == 1 and attn_w_q.shape[0] == 1
    d = x_prompt.shape[-1]
    p = dict(
        attn_norm_g=attn_norm_g, ffn_norm_g=ffn_norm_g,
        conv_w_pw1=conv_w_pw1[0].astype(bf16), conv_b_pw1=conv_b_pw1,
        conv_w_dw=jnp.broadcast_to(conv_w_dw[0][:, None, :], (CONV_WIDTH, V7X_SUBLANES, d)),
        conv_b_dw=conv_b_dw, conv_ln_g=conv_ln_g, conv_ln_b=conv_ln_b,
        conv_w_pw2=conv_w_pw2[0].astype(bf16), conv_b_pw2=conv_b_pw2,
        kv_norm_g=kv_norm_g.reshape(1, d), w_k=w_k.astype(bf16), w_v=w_v.astype(bf16),
        attn_w_q=attn_w_q[0].astype(bf16), attn_w_o=attn_w_o[0].astype(bf16),
        final_norm_g=final_norm_g.reshape(1, d),
    )
    lams = (jnp.concatenate([attn_lambda_q1, attn_lambda_k1, attn_lambda_q2, attn_lambda_k2], axis=0),
            attn_subln_g)
    hd = d // N_HEADS // 2
    bsz, t, _ = x_prompt.shape
    sb, st, _ = x_sample.shape
    mp, ms = bsz * t, sb * st
    tm = PROMPT_ROW_TILE
    assert mp % tm == 0 and 2 * (mp // tm) == sb

    x1_s, conv_s = _conv_layer(x_sample, state_conv[0], p, ms, st)
    x1_s, *w0 = _ffn(x1_s, ffn_norm_g[0:1], ffn_w_gate, ffn_w_up, ffn_w_down, p['final_norm_g'],
                     ms, FFN_TILE, False, layer=0)
    w1 = (ffn_w_gate[1].astype(bf16), ffn_w_up[1].astype(bf16), ffn_w_down[1].astype(bf16))
    k_s, v_s, q_s, _, _ = _project_qkv(x1_s, p, ms)
    paged = (q_s.reshape(sb, st, d), k_s.reshape(sb, st * 2 * N_HEADS, hd),
             v_s.reshape(sb, st * N_HEADS, 2 * hd), cache_k, cache_v, *lams)

    def ffn_attn(x, layer, w, final_norm, seq0):
        gains = jnp.concatenate([ffn_norm_g[layer:layer + 1], p['final_norm_g']], axis=0)
        return _ffn_attn(x, gains, *w, final_norm, page_table, seq0, *paged, tm=tm, tf=FUSED_FFN_TILE)

    conv0 = jnp.zeros((bsz, CONV_WIDTH - 1, d), x_prompt.dtype)
    x1_p, conv_p = _conv_layer(x_prompt, conv0, p, tm, 512)
    x1_p, o_s0 = ffn_attn(x1_p, 0, w0, False, 0)
    k_p, v_p, q_p, kb_p, vb_p = _project_qkv(x1_p, p, tm)
    o_p = _prompt_attn(q_p.reshape(bsz, t, d), kb_p.reshape(bsz, t, d), vb_p.reshape(bsz, t, d),
                       *lams, tq=512)
    x2_p = _oproj(o_p.reshape(mp, d), p['attn_w_o'], x1_p, tm, min(512, d))
    y_p, o_s1 = ffn_attn(x2_p, 1, w1, True, sb // 2)

    o_s = jnp.concatenate([o_s0, o_s1], axis=0).reshape(ms, d)
    x2_s = _oproj(o_s, p['attn_w_o'], x1_s, ms, min(512, d))
    y_s = _ffn(x2_s, ffn_norm_g[1:2], *w1, p['final_norm_g'], ms, FFN_TILE, True)

    def heads(k, v, b, tt):
        return k.reshape(b, tt, 2 * N_HEADS, hd), v.reshape(b, tt, N_HEADS, 2 * hd)

    return (y_p.reshape(bsz, t, d), y_s.reshape(sb, st, d), conv_p, conv_s,
            *heads(k_p, v_p, bsz, t), *heads(k_s, v_s, sb, st))
```
